```python
import math
import jax, jax.numpy as jnp
from jax import lax
import numpy as np

D_MODEL = 1024
BATCH = 2
SEQ = 8192
DEPTH = 2
DEC_BATCH = 128
DEC_SEQ = 8
PAST_LEN = 8192
PAGE_SIZE = 128

N_ATTN_LAYERS = (DEPTH + 1) // 2
N_CONV_LAYERS = DEPTH // 2
MLA_HEADS = 8
QK_NOPE = 64
QK_ROPE = 32
V_HEAD = 64
Q_LORA = 384
KV_LORA = 256
ROPE_BASE = 10000.0
MLA_OUT = MLA_HEADS * V_HEAD
DIFF_HEADS = 4
DIFF_HD = 64
DIFF_W = DIFF_HEADS * 2 * DIFF_HD
ATTN_OUT = MLA_OUT + DIFF_W
IN_SPLITS = (Q_LORA, Q_LORA + KV_LORA + QK_ROPE, Q_LORA + KV_LORA + QK_ROPE + DIFF_W, Q_LORA + KV_LORA + QK_ROPE + 2 * DIFF_W)
IN_COLS = Q_LORA + KV_LORA + QK_ROPE + 3 * DIFF_W
Q_BLOCK = 128
CONV_W = 31
D_CONV = D_MODEL
N_EXPERTS = 16
N_GROUPS = 4
E_PER_GROUP = N_EXPERTS // N_GROUPS
GROUP_SCORE_K = 2
TOP_K = 2
D_EXPERT = 512
EPS = 1e-6

kernel_name = 'mla_diffattn_conformer_grouped_moe_step'


def rmsnorm(x, g):
    xf = x.astype(jnp.float32)
    y = xf * lax.rsqrt(jnp.mean(xf * xf, axis=-1, keepdims=True) + EPS)
    return (y * g.astype(jnp.float32)).astype(x.dtype)


def tied_gain(g):
    return jnp.concatenate([g, g[QK_NOPE:]], axis=0)


def rope(x, pos):
    half = QK_ROPE // 2
    inv = ROPE_BASE ** (-jnp.arange(0, QK_ROPE, 2, dtype=jnp.float32) / QK_ROPE)
    ang = pos.astype(jnp.float32)[:, None] * inv[None, :]
    shape = (pos.shape[0],) + (1,) * (x.ndim - 3) + (half,)
    cos = jnp.cos(ang).reshape(shape).astype(x.dtype)
    sin = jnp.sin(ang).reshape(shape).astype(x.dtype)
    x1, x2 = x[..., :half], x[..., half:]
    return jnp.concatenate([x1 * cos - x2 * sin, x2 * cos + x1 * sin], axis=-1)


def alibi_slopes():
    return 2.0 ** (-8.0 * jnp.arange(1, DIFF_HEADS + 1, dtype=jnp.float32) / DIFF_HEADS)


def adaln(c, w, b):
    m = jax.nn.silu(c) @ w + b
    return [p[:, None, :] for p in jnp.split(m, 6, axis=-1)]


def modulate(x, g, shift, scale):
    return rmsnorm(x, g) * (1.0 + scale) + shift


def attn_project(h, pos, w_in, q_a_norm, w_q_b, kv_a_norm, mla_q_gain, diff_q_gain, diff_k_gain):
    n, t, _ = h.shape
    q_a, kv_a, dq, dk, dv = jnp.split(h @ w_in, IN_SPLITS, axis=-1)
    q = (rmsnorm(q_a, q_a_norm) @ w_q_b).reshape(n, t, MLA_HEADS, QK_NOPE + QK_ROPE)
    q = jnp.concatenate([q[..., :QK_NOPE], rope(q[..., QK_NOPE:], pos)], axis=-1)
    q = rmsnorm(q, tied_gain(mla_q_gain))
    ckv = rmsnorm(kv_a[..., :KV_LORA], kv_a_norm)
    kpe = rope(kv_a[..., KV_LORA:], pos)
    dq = rmsnorm(dq.reshape(n, t, DIFF_HEADS, 2, DIFF_HD), diff_q_gain)
    dk = rmsnorm(dk.reshape(n, t, DIFF_HEADS, 2, DIFF_HD), diff_k_gain)
    dv = dv.reshape(n, t, DIFF_HEADS, 2 * DIFF_HD)
    return q, ckv, kpe, dq, dk, dv


def mla_expand(ckv, kpe, w_kv_b, mla_k_gain):
    lead = ckv.shape[:-1]
    kv = (ckv @ w_kv_b).reshape(lead + (MLA_HEADS, QK_NOPE + V_HEAD))
    k_rope = jnp.broadcast_to(kpe[..., None, :], lead + (MLA_HEADS, QK_ROPE))
    k = rmsnorm(jnp.concatenate([kv[..., :QK_NOPE], k_rope], axis=-1), tied_gain(mla_k_gain))
    return k, kv[..., QK_NOPE:]


def mla_core(q, k, v, mask):
    s = jnp.einsum('nqhd,nkhd->nhqk', q, k).astype(jnp.float32) * (q.shape[-1] ** -0.5)
    p = jax.nn.softmax(jnp.where(mask, s, -jnp.inf), axis=-1)
    return jnp.einsum('nhqk,nkhd->nqhd', p.astype(v.dtype), v)


def diff_core(q, k, v, lam, mask, dist):
    s = jnp.einsum('nqhcd,nkhcd->nchqk', q, k).astype(jnp.float32) * (DIFF_HD ** -0.5)
    s = s - alibi_slopes()[:, None, None] * dist[None]
    p = jax.nn.softmax(jnp.where(mask, s, -jnp.inf), axis=-1)
    w = p[:, 0] - lam * p[:, 1]
    return jnp.einsum('nhqk,nkhd->nqhd', w.astype(v.dtype), v)


def diff_lambda_value(lp, lam_init):
    lp = lp.astype(jnp.float32)
    return jnp.exp(jnp.sum(lp[0] * lp[1])) - jnp.exp(jnp.sum(lp[2] * lp[3])) + lam_init


def attn_merge(om, od, subln, w_o, lam_init):
    n, t = om.shape[:2]
    od = rmsnorm(od, subln) * (1.0 - lam_init)
    return jnp.concatenate([om.reshape(n, t, MLA_OUT), od.reshape(n, t, DIFF_W)], axis=-1) @ w_o


def prompt_attention(q, k, v, dq, dk, dv, lam):
    n, t = q.shape[:2]
    kpos = jnp.arange(t)

    def block(i):
        s0 = i * Q_BLOCK
        qpos = s0 + jnp.arange(Q_BLOCK)
        mask = kpos[None, :] <= qpos[:, None]
        dist = (qpos[:, None] - kpos[None, :]).astype(jnp.float32)
        qb = lax.dynamic_slice_in_dim(q, s0, Q_BLOCK, axis=1)
        dqb = lax.dynamic_slice_in_dim(dq, s0, Q_BLOCK, axis=1)
        return mla_core(qb, k, v, mask), diff_core(dqb, dk, dv, lam, mask, dist)

    om, od = lax.map(block, jnp.arange(t // Q_BLOCK))
    om = jnp.moveaxis(om, 0, 1).reshape(n, t, MLA_HEADS, V_HEAD)
    od = jnp.moveaxis(od, 0, 1).reshape(n, t, DIFF_HEADS, 2 * DIFF_HD)
    return om, od


def sample_attention(q, ckv, kpe, dq, dk, dv, page_table, cache_lat, cache_kpe, cache_dk, cache_dv, a, w_kv_b, mla_k_gain, lam):
    past = page_table.shape[1] * PAGE_SIZE
    s = q.shape[1]
    qpos = past + jnp.arange(s)
    kpos = jnp.arange(past + s)
    mask = kpos[None, :] <= qpos[:, None]
    dist = (qpos[:, None] - kpos[None, :]).astype(jnp.float32)

    def one(args):
        q1, ckv1, kpe1, dq1, dk1, dv1, pt = args
        c_all = jnp.concatenate([cache_lat[a, pt].reshape(past, KV_LORA), ckv1], axis=0)
        p_all = jnp.concatenate([cache_kpe[a, pt].reshape(past, QK_ROPE), kpe1], axis=0)
        k, v = mla_expand(c_all, p_all, w_kv_b, mla_k_gain)
        om = mla_core(q1[None], k[None], v[None], mask)[0]
        dk_all = jnp.concatenate([cache_dk[a, pt].reshape(past, DIFF_HEADS, 2, DIFF_HD), dk1], axis=0)
        dv_all = jnp.concatenate([cache_dv[a, pt].reshape(past, DIFF_HEADS, 2 * DIFF_HD), dv1], axis=0)
        od = diff_core(dq1[None], dk_all[None], dv_all[None], lam, mask, dist)[0]
        return om, od

    return lax.map(one, (q, ckv, kpe, dq, dk, dv, page_table))


def conformer_conv(h, buf, pw1, pw1_b, dw, dw_b, cnorm, pw2, pw2_b):
    a, g = jnp.split(h @ pw1 + pw1_b, 2, axis=-1)
    u = a * jax.nn.sigmoid(g)
    full = jnp.concatenate([buf, u], axis=1)
    y = lax.conv_general_dilated(full, dw[:, None, :], window_strides=(1,), padding='VALID',
                                 dimension_numbers=('NWC', 'WIO', 'NWC'), feature_group_count=D_CONV) + dw_b
    y = jax.nn.silu(rmsnorm(y, cnorm))
    return y @ pw2 + pw2_b, full[:, full.shape[1] - (CONV_W - 1):]


def moe_ffn(h, router_w, router_bias, w_gate, w_up, w_down):
    n, t, d = h.shape
    tok = h.reshape(n * t, d)
    scores = jax.nn.sigmoid((tok @ router_w).astype(jnp.float32))
    sel = (scores + router_bias.astype(jnp.float32)).reshape(n * t, N_GROUPS, E_PER_GROUP)
    group_score = lax.top_k(sel, GROUP_SCORE_K)[0].sum(-1)
    grp = jnp.argmax(group_score, axis=-1)
    sel_in = jnp.take_along_axis(sel, grp[:, None, None], axis=1)[:, 0]
    _, loc = lax.top_k(sel_in, TOP_K)
    idx = grp[:, None] * E_PER_GROUP + loc
    wts = jnp.take_along_axis(scores, idx, axis=-1)
    wts = wts / wts.sum(-1, keepdims=True)
    gates = jnp.einsum('mk,mke->me', wts, jax.nn.one_hot(idx, N_EXPERTS, dtype=jnp.float32)).astype(h.dtype)
    out = jnp.zeros_like(tok)
    for e in range(N_EXPERTS):
        u = jax.nn.silu(tok @ w_gate[e]) * (tok @ w_up[e])
        out = out + gates[:, e:e + 1] * (u @ w_down[e])
    return out.reshape(n, t, d)


def setup_inputs(seed: int = 0) -> dict:
    key = jax.random.key(seed)
    ks = iter(jax.random.split(key, 48))
    f32 = jnp.float32

    def nrm(shape, scale=1.0):
        return jax.random.normal(next(ks), shape, f32) * scale

    def gain(shape):
        return 1.0 + nrm(shape, 0.02)

    n_pages = PAST_LEN // PAGE_SIZE
    n_used = DEC_BATCH * n_pages
    n_pool = n_used + n_used // 4
    A, C, L = N_ATTN_LAYERS, N_CONV_LAYERS, DEPTH
    inputs = {}
    inputs['x_prompt'] = nrm((BATCH, SEQ, D_MODEL))
    inputs['x_sample'] = nrm((DEC_BATCH, DEC_SEQ, D_MODEL))
    inputs['cache_mla_latent'] = nrm((A, n_pool, PAGE_SIZE, KV_LORA))
    inputs['cache_mla_kpe'] = nrm((A, n_pool, PAGE_SIZE, QK_ROPE))
    inputs['cache_diff_k'] = nrm((A, n_pool, PAGE_SIZE, DIFF_HEADS, 2 * DIFF_HD))
    inputs['cache_diff_v'] = nrm((A, n_pool, PAGE_SIZE, DIFF_HEADS, 2 * DIFF_HD))
    inputs['state_conv'] = nrm((C, DEC_BATCH, CONV_W - 1, D_CONV))
    inputs['page_table'] = jax.random.permutation(next(ks), n_pool)[:n_used].reshape(DEC_BATCH, n_pages).astype(jnp.int32)
    inputs['c_prompt'] = nrm((BATCH, D_MODEL))
    inputs['c_sample'] = nrm((DEC_BATCH, D_MODEL))
    inputs['norm1'] = gain((L, D_MODEL))
    inputs['norm2'] = gain((L, D_MODEL))
    inputs['ada_w'] = nrm((L, D_MODEL, 6 * D_MODEL), 0.5 * D_MODEL ** -0.5)
    inputs['ada_b'] = nrm((L, 6 * D_MODEL), 0.02)
    inputs['w_in'] = nrm((A, D_MODEL, IN_COLS), D_MODEL ** -0.5)
    inputs['q_a_norm'] = gain((A, Q_LORA))
    inputs['w_q_b'] = nrm((A, Q_LORA, MLA_HEADS * (QK_NOPE + QK_ROPE)), Q_LORA ** -0.5)
    inputs['kv_a_norm'] = gain((A, KV_LORA))
    inputs['w_kv_b'] = nrm((A, KV_LORA, MLA_HEADS * (QK_NOPE + V_HEAD)), KV_LORA ** -0.5)
    inputs['mla_q_gain'] = gain((A, QK_NOPE + QK_ROPE // 2))
    inputs['mla_k_gain'] = gain((A, QK_NOPE + QK_ROPE // 2))
    inputs['diff_q_gain'] = gain((A, DIFF_HD))
    inputs['diff_k_gain'] = gain((A, DIFF_HD))
    inputs['diff_lambda'] = nrm((A, 4, DIFF_HD), 0.1)
    inputs['diff_subln'] = gain((A, 2 * DIFF_HD))
    inputs['w_o'] = nrm((A, ATTN_OUT, D_MODEL), ATTN_OUT ** -0.5)
    inputs['conv_pw1'] = nrm((C, D_MODEL, 2 * D_CONV), D_MODEL ** -0.5)
    inputs['conv_pw1_b'] = nrm((C, 2 * D_CONV), 0.02)
    inputs['conv_dw'] = nrm((C, CONV_W, D_CONV), CONV_W ** -0.5)
    inputs['conv_dw_b'] = nrm((C, D_CONV), 0.02)
    inputs['conv_norm'] = gain((C, D_CONV))
    inputs['conv_pw2'] = nrm((C, D_CONV, D_MODEL), D_CONV ** -0.5)
    inputs['conv_pw2_b'] = nrm((C, D_MODEL), 0.02)
    inputs['router_w'] = nrm((D_MODEL, N_EXPERTS), D_MODEL ** -0.5)
    inputs['router_bias'] = nrm((N_EXPERTS,), 0.01)
    inputs['moe_w_gate'] = nrm((L, N_EXPERTS, D_MODEL, D_EXPERT), D_MODEL ** -0.5)
    inputs['moe_w_up'] = nrm((L, N_EXPERTS, D_MODEL, D_EXPERT), D_MODEL ** -0.5)
    inputs['moe_w_down'] = nrm((L, N_EXPERTS, D_EXPERT, D_MODEL), D_EXPERT ** -0.5)
    return inputs


def reference(x_prompt, x_sample, cache_mla_latent, cache_mla_kpe, cache_diff_k, cache_diff_v, state_conv, page_table,
              c_prompt, c_sample, norm1, norm2, ada_w, ada_b,
              w_in, q_a_norm, w_q_b, kv_a_norm, w_kv_b, mla_q_gain, mla_k_gain, diff_q_gain, diff_k_gain,
              diff_lambda, diff_subln, w_o,
              conv_pw1, conv_pw1_b, conv_dw, conv_dw_b, conv_norm, conv_pw2, conv_pw2_b,
              router_w, router_bias, moe_w_gate, moe_w_up, moe_w_down):
    xp, xs = x_prompt, x_sample
    pos_p = jnp.arange(xp.shape[1])
    pos_s = page_table.shape[1] * PAGE_SIZE + jnp.arange(xs.shape[1])
    lat_p, lat_s, kpe_p, kpe_s, dk_p, dk_s, dv_p, dv_s, cv_p, cv_s = [], [], [], [], [], [], [], [], [], []
    for l in range(DEPTH):
        shp1, scp1, gp1, shp2, scp2, gp2 = adaln(c_prompt, ada_w[l], ada_b[l])
        shs1, scs1, gs1, shs2, scs2, gs2 = adaln(c_sample, ada_w[l], ada_b[l])
        hp = modulate(xp, norm1[l], shp1, scp1)
        hs = modulate(xs, norm1[l], shs1, scs1)
        if l % 2 == 0:
            a = l // 2
            lam_init = 0.8 - 0.6 * math.exp(-0.3 * l)
            lam = diff_lambda_value(diff_lambda[a], lam_init)
            q, ckv, kpe, dq, dk, dv = attn_project(hp, pos_p, w_in[a], q_a_norm[a], w_q_b[a], kv_a_norm[a],
                                                   mla_q_gain[a], diff_q_gain[a], diff_k_gain[a])
            k, v = mla_expand(ckv, kpe, w_kv_b[a], mla_k_gain[a])
            om, od = prompt_attention(q, k, v, dq, dk, dv, lam)
            yp = attn_merge(om, od, diff_subln[a], w_o[a], lam_init)
            n, t = ckv.shape[:2]
            lat_p.append(ckv); kpe_p.append(kpe)
            dk_p.append(dk.reshape(n, t, DIFF_HEADS, 2 * DIFF_HD)); dv_p.append(dv)
            q, ckv, kpe, dq, dk, dv = attn_project(hs, pos_s, w_in[a], q_a_norm[a], w_q_b[a], kv_a_norm[a],
                                                   mla_q_gain[a], diff_q_gain[a], diff_k_gain[a])
            om, od = sample_attention(q, ckv, kpe, dq, dk, dv, page_table, cache_mla_latent, cache_mla_kpe,
                                      cache_diff_k, cache_diff_v, a, w_kv_b[a], mla_k_gain[a], lam)
            ys = attn_merge(om, od, diff_subln[a], w_o[a], lam_init)
            n, t = ckv.shape[:2]
            lat_s.append(ckv); kpe_s.append(kpe)
            dk_s.append(dk.reshape(n, t, DIFF_HEADS, 2 * DIFF_HD)); dv_s.append(dv)
        else:
            cidx = l // 2
            buf0 = jnp.zeros((xp.shape[0], CONV_W - 1, D_CONV), xp.dtype)
            yp, bp = conformer_conv(hp, buf0, conv_pw1[cidx], conv_pw1_b[cidx], conv_dw[cidx], conv_dw_b[cidx],
                                    conv_norm[cidx], conv_pw2[cidx], conv_pw2_b[cidx])
            ys, bs = conformer_conv(hs, state_conv[cidx], conv_pw1[cidx], conv_pw1_b[cidx], conv_dw[cidx], conv_dw_b[cidx],
                                    conv_norm[cidx], conv_pw2[cidx], conv_pw2_b[cidx])
            cv_p.append(bp); cv_s.append(bs)
        xp = xp + gp1 * yp
        xs = xs + gs1 * ys
        hp = modulate(xp, norm2[l], shp2, scp2)
        hs = modulate(xs, norm2[l], shs2, scs2)
        xp = xp + gp2 * moe_ffn(hp, router_w, router_bias, moe_w_gate[l], moe_w_up[l], moe_w_down[l])
        xs = xs + gs2 * moe_ffn(hs, router_w, router_bias, moe_w_gate[l], moe_w_up[l], moe_w_down[l])
    return (xp, xs, jnp.stack(lat_p), jnp.stack(lat_s), jnp.stack(kpe_p), jnp.stack(kpe_s),
            jnp.stack(dk_p), jnp.stack(dk_s), jnp.stack(dv_p), jnp.stack(dv_s), jnp.stack(cv_p), jnp.stack(cv_s))
```

```python
import functools
import math

import jax
import jax.numpy as jnp
from jax import lax
from jax.experimental import pallas as pl
from jax.experimental.pallas import tpu as pltpu

F32 = jnp.float32
BF16 = jnp.bfloat16

D_MODEL = 1024
PAGE_SIZE = 128
MLA_HEADS = 8
QK_NOPE = 64
QK_ROPE = 32
QK_DIM = QK_NOPE + QK_ROPE
V_HEAD = 64
Q_LORA = 384
KV_LORA = 256
ROPE_BASE = 10000.0
MLA_OUT = MLA_HEADS * V_HEAD
DIFF_HEADS = 4
DIFF_HD = 64
DIFF_W = DIFF_HEADS * 2 * DIFF_HD
CONV_W = 31
N_EXPERTS = 16
N_GROUPS = 4
E_PER_GROUP = N_EXPERTS // N_GROUPS
D_EXPERT = 512
EPS = 1e-6
LOG2E = math.log2(math.e)

LANES = 128
HEAD_PAD = 128
VMEM_LIMIT = 48 * 1024 * 1024


def _cparams(*sem):
    return pltpu.CompilerParams(dimension_semantics=sem, vmem_limit_bytes=VMEM_LIMIT)


def _dot(a, b):
    return jnp.dot(a.astype(BF16), b.astype(BF16), preferred_element_type=F32)


def _dot_nt(a, b):
    return lax.dot_general(a.astype(BF16), b.astype(BF16), (((1,), (1,)), ((), ())),
                           preferred_element_type=F32)


def _dot_tn(a, b):
    return lax.dot_general(a.astype(BF16), b.astype(BF16), (((0,), (0,)), ((), ())),
                           preferred_element_type=F32)


def _rms(x, n):
    return x * lax.rsqrt(jnp.sum(x * x, axis=-1, keepdims=True) * (1.0 / n) + EPS)


def _lane_iota(shape):
    return lax.broadcasted_iota(jnp.int32, shape, len(shape) - 1)


def _adaln_kernel(c_ref, w_ref, b_ref, o_ref):
    c = c_ref[...]
    s = c * jax.nn.sigmoid(c)
    o_ref[0] = _dot(s, w_ref[0]) + b_ref[0]


def _adaln(c_all, ada_w, ada_b):
    n_layers, d, d6 = ada_w.shape
    rows = c_all.shape[0]
    tn = 1536
    return pl.pallas_call(
        _adaln_kernel,
        grid=(n_layers, d6 // tn),
        in_specs=[pl.BlockSpec((rows, d), lambda l, j: (0, 0)),
                  pl.BlockSpec((1, d, tn), lambda l, j: (l, 0, j)),
                  pl.BlockSpec((1, 1, tn), lambda l, j: (l, 0, j))],
        out_specs=pl.BlockSpec((1, rows, tn), lambda l, j: (l, 0, j)),
        out_shape=jax.ShapeDtypeStruct((n_layers, rows, d6), F32),
        compiler_params=_cparams("arbitrary", "arbitrary"),
        name="adaln",
    )(c_all, ada_w, ada_b.reshape(n_layers, 1, d6))


def _modulate(x3, g, shift, scale):
    nb, s, d = x3.shape
    y = _rms(x3, d) * g[None]
    y = y * (1.0 + scale) + shift
    return y.reshape(nb * s, d)


def _bcast_rows(tab, nb):
    s, l = tab.shape
    return jnp.broadcast_to(tab[None], (nb, s, l)).reshape(nb * s, l)


def _rope_block(x, cos_t, sin_a, sin_b, half):
    n = x.shape[-1]
    return x * cos_t + pltpu.roll(x, n - half, 1) * sin_a + pltpu.roll(x, half, 1) * sin_b


def _router_gates_t(h2, rwt, rbt):
    tm = h2.shape[0]
    h_hi = h2.astype(BF16)
    h_lo = (h2 - h_hi.astype(F32)).astype(BF16)
    by_hi = _dot_nt(rwt, h_hi)
    logits = by_hi[:N_EXPERTS] + by_hi[N_EXPERTS:] + _dot_nt(rwt[:N_EXPERTS], h_lo)
    scores = jax.nn.sigmoid(logits)
    sel = scores + rbt
    a = [sel[i * N_GROUPS:(i + 1) * N_GROUPS] for i in range(E_PER_GROUP)]
    sc = [scores[i * N_GROUPS:(i + 1) * N_GROUPS] for i in range(E_PER_GROUP)]
    member = []
    for i in range(E_PER_GROUP):
        rank = jnp.zeros((N_GROUPS, tm), F32)
        for j in range(E_PER_GROUP):
            if j != i:
                ahead = (a[j] >= a[i]) if j < i else (a[j] > a[i])
                rank = rank + jnp.where(ahead, 1.0, 0.0)
        member.append(rank < 2.0)
    gscore = jnp.zeros((N_GROUPS, tm), F32)
    for i in range(E_PER_GROUP):
        gscore = gscore + jnp.where(member[i], a[i], 0.0)
    best = gscore[0:1]
    best_g = jnp.zeros((1, tm), jnp.int32)
    for g in range(1, N_GROUPS):
        better = gscore[g:g + 1] > best
        best = jnp.where(better, gscore[g:g + 1], best)
        best_g = jnp.where(better, g, best_g)
    in_group = lax.broadcasted_iota(jnp.int32, (N_GROUPS, tm), 0) == best_g
    picked = [jnp.where(in_group, jnp.where(member[i], sc[i], 0.0), 0.0) for i in range(E_PER_GROUP)]
    total = picked[0]
    for i in range(1, E_PER_GROUP):
        total = total + picked[i]
    inv = 1.0 / jnp.sum(total, axis=0, keepdims=True)
    return jnp.concatenate([p * inv for p in picked], axis=0)


def _residual_and_route(x3, y, gate1, g2, sh2, sc2, rwt, rbt, x1_ref, h2_ref, gates_ref):
    nb, s, d = x3.shape
    x1 = x3 + gate1 * y.reshape(nb, s, d)
    x1_ref[...] = x1
    h2 = _modulate(x1, g2, sh2, sc2)
    h2_ref[...] = h2.astype(BF16).reshape(nb, s, d)
    gates_ref[...] = _router_gates_t(h2, rwt, rbt)


def _gates_dense(gates_t):
    m = gates_t.shape[1]
    return gates_t.reshape(E_PER_GROUP, N_GROUPS, m).transpose(2, 1, 0).reshape(m, N_EXPERTS)


def _router_layout(router_w, router_bias):
    d = router_w.shape[0]
    rwt = router_w.T.reshape(N_GROUPS, E_PER_GROUP, d).transpose(1, 0, 2).reshape(N_EXPERTS, d)
    rbt = router_bias.reshape(N_GROUPS, E_PER_GROUP).T.reshape(N_EXPERTS, 1)
    hi = rwt.astype(BF16)
    lo = (rwt - hi.astype(F32)).astype(BF16)
    return jnp.concatenate([hi, lo], axis=0), rbt


IN_PAD = Q_LORA + KV_LORA + LANES + 3 * DIFF_W
C_QA, C_CKV, C_KPE, C_DQ, C_DK, C_DV = 0, Q_LORA, Q_LORA + KV_LORA, 768, 1280, 1792


def _pair_norm(x, gain2):
    outs = []
    for h in range(DIFF_HEADS):
        blk = x[:, h * LANES:(h + 1) * LANES]
        lo = _lane_iota(blk.shape) < DIFF_HD
        sq = blk * blk
        s_lo = jnp.sum(jnp.where(lo, sq, 0.0), axis=-1, keepdims=True)
        s_all = jnp.sum(sq, axis=-1, keepdims=True)
        r_lo = lax.rsqrt(s_lo * (1.0 / DIFF_HD) + EPS)
        r_hi = lax.rsqrt((s_all - s_lo) * (1.0 / DIFF_HD) + EPS)
        outs.append(blk * jnp.where(lo, r_lo, r_hi) * gain2)
    return jnp.concatenate(outs, axis=1)


def _proj_kernel(prompt, x_ref, sh_ref, sc_ref, g1_ref, win_ref, qan_ref, wqb_ref, kvn_ref, wkv_ref,
                 qg_ref, kg_ref, dqg_ref, dkg_ref, tab_ref, *outs):
    if prompt:
        q_ref, ckv_ref, kpe_ref, dq_ref, dk_ref, dv_ref, k_ref, v_ref, dkb_ref, dvb_ref = outs
    else:
        q_ref, ckv_ref, kpe_ref, dq_ref, dk_ref, dv_ref = outs
    nb, s, _ = x_ref.shape
    tm = nb * s
    h = _modulate(x_ref[...], g1_ref[...], sh_ref[...], sc_ref[...])
    z = _dot(h, win_ref[...])
    tab = tab_ref[...]
    qcos, qsa, qsb, kcos, ksa, ksb = [_bcast_rows(tab[i], nb) for i in range(6)]

    ckv = _rms(z[:, C_CKV:C_CKV + KV_LORA], KV_LORA) * kvn_ref[...]
    ckv_ref[...] = ckv.reshape(nb, s, KV_LORA)
    kpe = _rope_block(z[:, C_KPE:C_KPE + LANES], kcos, ksa, ksb, QK_ROPE // 2)
    kpe_ref[...] = kpe[:, :QK_ROPE].reshape(nb, s, QK_ROPE)

    qa = _rms(z[:, C_QA:C_QA + Q_LORA], Q_LORA) * qan_ref[...]
    qh = _dot(qa, wqb_ref[...])
    q_scale = QK_DIM ** -0.5 * LOG2E if prompt else 1.0
    q_out = []
    for hd in range(MLA_HEADS):
        blk = qh[:, hd * HEAD_PAD:(hd + 1) * HEAD_PAD]
        blk = _rope_block(blk, qcos, qsa, qsb, QK_ROPE // 2)
        q_out.append(_rms(blk, QK_DIM) * (qg_ref[...] * q_scale))
    q_ref[...] = jnp.concatenate(q_out, axis=1).astype(q_ref.dtype).reshape(nb, s, MLA_HEADS * HEAD_PAD)

    d_scale = DIFF_HD ** -0.5 * LOG2E if prompt else 1.0
    dq = _pair_norm(z[:, C_DQ:C_DQ + DIFF_W], dqg_ref[...] * d_scale)
    dq_ref[...] = dq.astype(dq_ref.dtype).reshape(nb, s, DIFF_W)
    dk = _pair_norm(z[:, C_DK:C_DK + DIFF_W], dkg_ref[...])
    dk_ref[...] = dk.reshape(nb, s, DIFF_W)
    dv = z[:, C_DV:C_DV + DIFF_W]
    dv_ref[...] = dv.reshape(nb, s, DIFF_W)

    if prompt:
        dkb_ref[...] = dk.astype(BF16).reshape(nb, s, DIFF_W)
        dvb_ref[...] = dv.astype(BF16).reshape(nb, s, DIFF_W)
        kv = _dot(ckv, wkv_ref[...])
        kpe_sh = pltpu.roll(kpe, QK_NOPE, 1)
        k_out = []
        for hd in range(MLA_HEADS):
            blk = kv[:, hd * HEAD_PAD:(hd + 1) * HEAD_PAD] + kpe_sh
            k_out.append(_rms(blk, QK_DIM) * kg_ref[...])
        k_ref[...] = jnp.concatenate(k_out, axis=1).astype(BF16).reshape(nb, s, MLA_HEADS * HEAD_PAD)
        v_ref[...] = kv[:, MLA_HEADS * HEAD_PAD:].astype(BF16).reshape(nb, s, MLA_OUT)


def _attn_project(prompt, x3, shift, scale, g1, wts, tab, nb, s):
    nb_tot, s_tot, d = x3.shape
    grid = (nb_tot // nb, s_tot // s)
    tok = lambda w: pl.BlockSpec((nb, s, w), lambda i, j: (i, j, 0))
    mod = pl.BlockSpec((nb, 1, d), lambda i, j: (i, 0, 0))
    full = lambda a: pl.BlockSpec(a.shape, lambda i, j: (0,) * a.ndim)
    qdt = BF16 if prompt else F32
    out_shape = [jax.ShapeDtypeStruct((nb_tot, s_tot, MLA_HEADS * HEAD_PAD), qdt),
                 jax.ShapeDtypeStruct((nb_tot, s_tot, KV_LORA), F32),
                 jax.ShapeDtypeStruct((nb_tot, s_tot, QK_ROPE), F32),
                 jax.ShapeDtypeStruct((nb_tot, s_tot, DIFF_W), qdt),
                 jax.ShapeDtypeStruct((nb_tot, s_tot, DIFF_W), F32),
                 jax.ShapeDtypeStruct((nb_tot, s_tot, DIFF_W), F32)]
    out_specs = [tok(MLA_HEADS * HEAD_PAD), tok(KV_LORA), tok(QK_ROPE), tok(DIFF_W), tok(DIFF_W), tok(DIFF_W)]
    if prompt:
        out_shape += [jax.ShapeDtypeStruct((nb_tot, s_tot, MLA_HEADS * HEAD_PAD), BF16),
                      jax.ShapeDtypeStruct((nb_tot, s_tot, MLA_OUT), BF16),
                      jax.ShapeDtypeStruct((nb_tot, s_tot, DIFF_W), BF16),
                      jax.ShapeDtypeStruct((nb_tot, s_tot, DIFF_W), BF16)]
        out_specs += [tok(MLA_HEADS * HEAD_PAD), tok(MLA_OUT), tok(DIFF_W), tok(DIFF_W)]
    args = (g1, wts["w_in"], wts["q_a_norm"], wts["w_q_b"], wts["kv_a_norm"], wts["w_kv"],
            wts["q_gain"], wts["k_gain"], wts["dq_gain"], wts["dk_gain"])
    return pl.pallas_call(
        functools.partial(_proj_kernel, prompt),
        grid=grid,
        in_specs=[tok(d), mod, mod] + [full(a) for a in args]
                 + [pl.BlockSpec((6, s, LANES), lambda i, j: (0, j, 0))],
        out_specs=out_specs,
        out_shape=out_shape,
        compiler_params=_cparams("arbitrary", "arbitrary"),
        name="attn_project_prompt" if prompt else "attn_project_sample",
    )(x3, shift, scale, *args, tab)


ATT_BLK = 1024
ATT_SUB = 512


def _flash_step(q, k, v, m, acc, mask=None):
    s = _dot_nt(q, k)
    if mask is not None:
        s = jnp.where(mask, s, -jnp.inf)
    m_new = jnp.maximum(m, jnp.max(s, axis=-1, keepdims=True))
    alpha = jnp.exp2(m - m_new)
    p = jnp.exp2(s - m_new)
    return m_new, alpha * acc + _dot(p, v)


def _causal_mask(n):
    return lax.broadcasted_iota(jnp.int32, (n, n), 1) <= lax.broadcasted_iota(jnp.int32, (n, n), 0)


def _flash_init(rows, width):
    return jnp.full((rows, 1), -jnp.inf, F32), jnp.zeros((rows, width), F32)


def _flash_diagonal(q, kv_rows, carry, blk, sub):
    m, acc = carry
    ms, accs = [], []
    tri = _causal_mask(sub)
    for r in range(blk // sub):
        rows = slice(r * sub, (r + 1) * sub)
        c_r = (m[rows], acc[rows])
        for c in range(r + 1):
            c_r = _flash_step(q[rows], *kv_rows(c), *c_r, mask=tri if c == r else None)
        ms.append(c_r[0])
        accs.append(c_r[1])
    return jnp.concatenate(ms, axis=0), jnp.concatenate(accs, axis=0)


def _mla_prompt_kernel(blk, sub, q_ref, k_ref, v_ref, o_ref):
    i = pl.program_id(2)
    heads = (0, 1)
    qs = [q_ref[0, :, hh * HEAD_PAD:(hh + 1) * HEAD_PAD] for hh in heads]

    def kv(start, size, hh):
        rows = pl.ds(pl.multiple_of(start, size), size)
        k = k_ref[0, rows, hh * HEAD_PAD:(hh + 1) * HEAD_PAD]
        vb = v_ref[0, rows, :]
        lane = _lane_iota(vb.shape)
        own = (lane < V_HEAD) if hh == 0 else (lane >= V_HEAD)
        one_at = V_HEAD if hh == 0 else 0
        v = jnp.where(own, vb, jnp.where(lane == one_at, 1.0, 0.0).astype(vb.dtype))
        return k, v

    def body(j, carry):
        return tuple(_flash_step(qs[hh], *kv(j * blk, blk, hh), *carry[hh]) for hh in heads)

    init = _flash_init(blk, 2 * V_HEAD)
    carry = lax.fori_loop(0, i, body, (init, init))
    accs = [_flash_diagonal(qs[hh], lambda c, hh=hh: kv(i * blk + c * sub, sub, hh), carry[hh], blk, sub)[1]
            for hh in heads]
    lane = _lane_iota(accs[0].shape)
    out = jnp.where(lane < V_HEAD, accs[0] / accs[0][:, V_HEAD:V_HEAD + 1], accs[1] / accs[1][:, 0:1])
    o_ref[0] = out.astype(o_ref.dtype)


def _mla_prompt_attention(q, k, v):
    n, t, _ = q.shape
    blk = min(ATT_BLK, t)
    sub = min(ATT_SUB, blk)
    return pl.pallas_call(
        functools.partial(_mla_prompt_kernel, blk, sub),
        grid=(n, MLA_HEADS // 2, t // blk),
        in_specs=[pl.BlockSpec((1, blk, 2 * HEAD_PAD), lambda b, h, i: (b, i, h)),
                  pl.BlockSpec((1, t, 2 * HEAD_PAD), lambda b, h, i: (b, 0, h)),
                  pl.BlockSpec((1, t, 2 * V_HEAD), lambda b, h, i: (b, 0, h))],
        out_specs=pl.BlockSpec((1, blk, 2 * V_HEAD), lambda b, h, i: (b, i, h)),
        out_shape=jax.ShapeDtypeStruct((n, t, MLA_OUT), BF16),
        compiler_params=_cparams("arbitrary", "arbitrary", "arbitrary"),
        name="mla_prompt_attention",
    )(q, k, v)


def _diff_lambda(lp_ref, lam_init):
    lp = lp_ref[...]
    a = jnp.sum(lp[0:1] * lp[1:2], axis=-1, keepdims=True)
    b = jnp.sum(lp[2:3] * lp[3:4], axis=-1, keepdims=True)
    return jnp.exp(a) - jnp.exp(b) + lam_init


ALIBI_SPLIT = 3


def _alibi_tables(t):
    kpos = jnp.arange(t)
    a, b = (kpos // 64).astype(F32), (kpos % 64).astype(F32)
    pad = jnp.zeros((t, LANES - 2 * ALIBI_SPLIT), F32)
    k_tab = jnp.concatenate([a[:, None]] * ALIBI_SPLIT + [b[:, None]] * ALIBI_SPLIT + [pad], axis=1).astype(BF16)
    pieces, rest = [], jnp.asarray(LOG2E, F32)
    for _ in range(ALIBI_SPLIT):
        p = rest.astype(BF16).astype(F32)
        pieces.append(p)
        rest = rest - p
    c = jnp.stack(pieces)
    slopes = 2.0 ** (-8.0 * jnp.arange(1, DIFF_HEADS + 1, dtype=F32) / DIFF_HEADS)
    q_tab = jnp.concatenate([64.0 * slopes[:, None] * c[None], slopes[:, None] * c[None],
                             jnp.zeros((DIFF_HEADS, LANES - 2 * ALIBI_SPLIT), F32)], axis=1)
    return k_tab, q_tab[:, None, :].astype(BF16)


def _diff_prompt_kernel(lam_init, blk, sub, q_ref, k_ref, v_ref, kt_ref, qt_ref, lp_ref, sub_ref, o_ref):
    i = pl.program_id(2)
    width = 2 * DIFF_HD
    q_all = q_ref[0]
    lo = _lane_iota(q_all.shape) < DIFF_HD
    q_pos = jnp.broadcast_to(qt_ref[0], q_all.shape)
    qs = (jnp.concatenate([jnp.where(lo, q_all, 0), q_pos], axis=1),
          jnp.concatenate([jnp.where(lo, 0, q_all), q_pos], axis=1))

    def kv(start, size):
        rows = pl.ds(pl.multiple_of(start, size), size)
        ones = jnp.where(_lane_iota((size, LANES)) == 0, 1.0, 0.0).astype(BF16)
        k = jnp.concatenate([k_ref[0, rows, :], kt_ref[rows, :]], axis=1)
        v = jnp.concatenate([v_ref[0, rows, :], ones], axis=1)
        return k, v

    def body(j, carry):
        k, v = kv(j * blk, blk)
        return tuple(_flash_step(qs[c], k, v, *carry[c]) for c in range(2))

    init = _flash_init(blk, width + LANES)
    carry = lax.fori_loop(0, i, body, (init, init))
    a0, a1 = [_flash_diagonal(qs[c], lambda r: kv(i * blk + r * sub, sub), carry[c], blk, sub)[1] for c in range(2)]
    lam = _diff_lambda(lp_ref, lam_init)
    od = a0[:, :width] / a0[:, width:width + 1] - lam * (a1[:, :width] / a1[:, width:width + 1])
    od = _rms(od, width) * sub_ref[...] * (1.0 - lam_init)
    o_ref[0] = od.astype(o_ref.dtype)


def _diff_prompt_attention(dq, dk, dv, diff_lambda, subln, lam_init):
    n, t, _ = dq.shape
    blkw = 2 * DIFF_HD
    blk = min(ATT_BLK, t)
    sub = min(ATT_SUB, blk)
    k_tab, q_tab = _alibi_tables(t)
    return pl.pallas_call(
        functools.partial(_diff_prompt_kernel, lam_init, blk, sub),
        grid=(n, DIFF_HEADS, t // blk),
        in_specs=[pl.BlockSpec((1, blk, blkw), lambda b, h, i: (b, i, h)),
                  pl.BlockSpec((1, t, blkw), lambda b, h, i: (b, 0, h)),
                  pl.BlockSpec((1, t, blkw), lambda b, h, i: (b, 0, h)),
                  pl.BlockSpec((t, LANES), lambda b, h, i: (0, 0)),
                  pl.BlockSpec((1, 1, LANES), lambda b, h, i: (h, 0, 0)),
                  pl.BlockSpec(diff_lambda.shape, lambda b, h, i: (0, 0)),
                  pl.BlockSpec(subln.shape, lambda b, h, i: (0, 0))],
        out_specs=pl.BlockSpec((1, blk, blkw), lambda b, h, i: (b, i, h)),
        out_shape=jax.ShapeDtypeStruct((n, t, DIFF_W), BF16),
        compiler_params=_cparams("arbitrary", "arbitrary", "arbitrary"),
        name="diff_prompt_attention",
    )(dq, dk, dv, k_tab, q_tab, diff_lambda, subln)


PAGES_PER_CHUNK = 16
N_SLOTS = 2


def _sample_attn_kernel(lam_init, layer, n_chunks, pt_ref,
                        qrow_ref, dqrow_ref, nckv_ref, nkpe_ref, ndk_ref, ndv_ref,
                        lat_hbm, kpe_hbm, dk_hbm, dv_hbm,
                        wkb_ref, wv_ref, kg_ref, lp_ref, sub_ref,
                        om_ref, od_ref,
                        lat_buf, kpe_buf, dk_buf, dv_buf, sems):
    b = pl.program_id(0)
    nb = pl.num_programs(0)
    ppc = PAGES_PER_CHUNK
    n_new = nckv_ref.shape[1]
    rows = MLA_HEADS * n_new
    hrows = PAGE_SIZE * DIFF_HEADS

    def copies(seq, chunk, slot):
        out = []
        for p in range(ppc):
            page = pt_ref[seq, chunk * ppc + p]
            out.append(pltpu.make_async_copy(lat_hbm.at[layer, page], lat_buf.at[slot, p], sems.at[0, slot]))
            out.append(pltpu.make_async_copy(kpe_hbm.at[layer, page], kpe_buf.at[slot, p], sems.at[1, slot]))
            out.append(pltpu.make_async_copy(dk_hbm.at[layer, page], dk_buf.at[slot, pl.ds(p * hrows, hrows)],
                                             sems.at[2, slot]))
            out.append(pltpu.make_async_copy(dv_hbm.at[layer, page], dv_buf.at[slot, pl.ds(p * hrows, hrows)],
                                             sems.at[3, slot]))
        return out

    @pl.when(b == 0)
    def _():
        for c in copies(0, 0, 0):
            c.start()

    qrow = qrow_ref[0]
    qt = qrow * (kg_ref[...] * (QK_DIM ** -0.5))
    lane = _lane_iota(qt.shape)
    qn = jnp.where(lane < QK_NOPE, qt, 0.0)
    qn2 = qn + pltpu.roll(qn, QK_NOPE, 1)
    qbd = jnp.concatenate([qn2] * (MLA_HEADS * QK_NOPE // LANES), axis=1)
    col = _lane_iota(qbd.shape)
    row = lax.broadcasted_iota(jnp.int32, qbd.shape, 0)
    blockdiag = col // QK_NOPE == row // n_new
    qbd = jnp.where(blockdiag, qbd, 0.0)
    qlat = _dot_nt(qbd, wkb_ref[...]).astype(BF16)
    qrope = pltpu.roll(qt, LANES - QK_NOPE, 1)[:, :QK_ROPE].astype(BF16)
    ind = jnp.where(blockdiag, 1.0, 0.0).astype(BF16)

    dqrow = dqrow_ref[0] * (DIFF_HD ** -0.5)
    dq2 = jnp.concatenate([dqrow, dqrow], axis=1)
    dqbd = jnp.concatenate([dq2] * (DIFF_W // LANES), axis=1)
    dqbd = jnp.where(blockdiag, dqbd, 0.0).astype(BF16)
    hd_col = lax.broadcasted_iota(jnp.int32, (rows, 1), 0) // (2 * n_new) + 1
    slope_col = jnp.exp2(-8.0 * hd_col.astype(F32) / DIFF_HEADS)

    def softmax_step(s, m, l):
        m_new = jnp.maximum(m, jnp.max(s, axis=-1, keepdims=True))
        alpha = jnp.exp(m - m_new)
        p = jnp.exp(s - m_new)
        return p, alpha, m_new, alpha * l + jnp.sum(p, axis=-1, keepdims=True)

    def chunk_update(state, ckv, kpe_t, dk, dv, kpos, mask):
        (m_a, l_a, acc_a), (m_d, l_d, acc_d) = state
        ckv_b = ckv.astype(BF16)
        kn = jnp.dot(ckv_b, wkb_ref[...], preferred_element_type=F32)
        ss = _dot_nt(ind, kn * kn) + jnp.sum(kpe_t * kpe_t, axis=0, keepdims=True)
        r = lax.rsqrt(ss * (1.0 / QK_DIM) + EPS)
        s = (_dot_nt(qlat, ckv_b) + _dot(qrope, kpe_t)) * r
        if mask is not None:
            s = jnp.where(mask, s, -jnp.inf)
        p, alpha, m_a, l_a = softmax_step(s, m_a, l_a)
        acc_a = acc_a * alpha + _dot(p, ckv_b)

        sd = _dot_nt(dqbd, dk) + slope_col * kpos
        if mask is not None:
            sd = jnp.where(mask, sd, -jnp.inf)
        pd, alpha_d, m_d, l_d = softmax_step(sd, m_d, l_d)
        acc_d = acc_d * alpha_d + _dot(pd, dv)
        return (m_a, l_a, acc_a), (m_d, l_d, acc_d)

    def init(width):
        return (jnp.full((rows, 1), -jnp.inf, F32), jnp.zeros((rows, 1), F32), jnp.zeros((rows, width), F32))

    def heads_to_lanes(buf, slot, n):
        return jnp.concatenate([buf[slot, pl.ds(h, n, stride=DIFF_HEADS), :] for h in range(DIFF_HEADS)], axis=1)

    state = (init(KV_LORA), init(DIFF_W))
    kc = ppc * PAGE_SIZE
    kpos_c = lax.broadcasted_iota(jnp.int32, (1, kc), 1).astype(F32)
    for c in range(n_chunks):
        slot = c % N_SLOTS
        nxt = (c + 1) % N_SLOTS
        if c + 1 < n_chunks:
            for cp in copies(b, c + 1, nxt):
                cp.start()
        else:
            @pl.when(b + 1 < nb)
            def _():
                for cp in copies(b + 1, 0, nxt):
                    cp.start()
        for cp in copies(b, c, slot):
            cp.wait()
        kpe_t = jnp.concatenate([kpe_buf[slot, p] for p in range(ppc)], axis=1)
        state = chunk_update(
            state,
            lat_buf[slot].reshape(kc, KV_LORA), kpe_t,
            heads_to_lanes(dk_buf, slot, kc), heads_to_lanes(dv_buf, slot, kc),
            kpos_c + float(c * kc), None)

    past = n_chunks * kc
    key_t = lax.broadcasted_iota(jnp.int32, (rows, n_new), 1)
    qry_t = lax.broadcasted_iota(jnp.int32, (rows, n_new), 0) % n_new
    kpos_n = lax.broadcasted_iota(jnp.int32, (1, n_new), 1).astype(F32) + float(past)
    state = chunk_update(state, nckv_ref[0], nkpe_ref[0], ndk_ref[0], ndv_ref[0], kpos_n, key_t <= qry_t)

    (_, l_a, acc_a), (_, l_d, acc_d) = state
    ov = _dot(acc_a / l_a, wv_ref[...])
    om = jnp.concatenate([ov[h * n_new:(h + 1) * n_new, h * V_HEAD:(h + 1) * V_HEAD]
                          for h in range(MLA_HEADS)], axis=1)
    om_ref[0] = om.astype(om_ref.dtype)

    lam = _diff_lambda(lp_ref, lam_init)
    on = acc_d / l_d
    ods = []
    for h in range(DIFF_HEADS):
        r0 = (2 * h) * n_new
        o0 = on[r0:r0 + n_new, h * LANES:(h + 1) * LANES]
        o1 = on[r0 + n_new:r0 + 2 * n_new, h * LANES:(h + 1) * LANES]
        o = o0 - lam * o1
        ods.append(_rms(o, 2 * DIFF_HD) * sub_ref[...] * (1.0 - lam_init))
    od_ref[0] = jnp.concatenate(ods, axis=1).astype(od_ref.dtype)


def _sample_attention(layer, page_table, qrow, dqrow, nckv, nkpe, ndk, ndv, lat, kpe, dk, dv,
                      wkb, wv, k_gain, diff_lambda, subln, lam_init):
    nseq, n_pages = page_table.shape
    n_new = nckv.shape[1]
    n_chunks = n_pages // PAGES_PER_CHUNK
    ppc = PAGES_PER_CHUNK
    seq = lambda a: pl.BlockSpec((1,) + a.shape[1:], lambda b, pt: (b,) + (0,) * (a.ndim - 1))
    full = lambda a: pl.BlockSpec(a.shape, lambda b, pt: (0,) * a.ndim)
    hbm = pl.BlockSpec(memory_space=pl.ANY)
    grid_spec = pltpu.PrefetchScalarGridSpec(
        num_scalar_prefetch=1,
        grid=(nseq,),
        in_specs=[seq(qrow), seq(dqrow), seq(nckv), seq(nkpe), seq(ndk), seq(ndv), hbm, hbm, hbm, hbm,
                  full(wkb), full(wv), full(k_gain), full(diff_lambda), full(subln)],
        out_specs=[pl.BlockSpec((1, n_new, MLA_OUT), lambda b, pt: (b, 0, 0)),
                   pl.BlockSpec((1, n_new, DIFF_W), lambda b, pt: (b, 0, 0))],
        scratch_shapes=[pltpu.VMEM((N_SLOTS, ppc, PAGE_SIZE, KV_LORA), F32),
                        pltpu.VMEM((N_SLOTS, ppc, QK_ROPE, PAGE_SIZE), F32),
                        pltpu.VMEM((N_SLOTS, ppc * PAGE_SIZE * DIFF_HEADS, 2 * DIFF_HD), F32),
                        pltpu.VMEM((N_SLOTS, ppc * PAGE_SIZE * DIFF_HEADS, 2 * DIFF_HD), F32),
                        pltpu.SemaphoreType.DMA((4, N_SLOTS))],
    )
    return pl.pallas_call(
        functools.partial(_sample_attn_kernel, lam_init, layer, n_chunks),
        grid_spec=grid_spec,
        out_shape=[jax.ShapeDtypeStruct((nseq, n_new, MLA_OUT), BF16),
                   jax.ShapeDtypeStruct((nseq, n_new, DIFF_W), BF16)],
        compiler_params=_cparams("arbitrary"),
        name="sample_attention",
    )(page_table, qrow, dqrow, nckv, nkpe, ndk, ndv, lat, kpe, dk, dv, wkb, wv, k_gain, diff_lambda, subln)


def _merge_kernel(x_ref, om_ref, od_ref, wo_ref, gate1_ref, g2_ref, sh2_ref, sc2_ref, rw_ref, rb_ref,
                  x1_ref, h2_ref, gates_ref):
    nb, s, d = x_ref.shape
    om = om_ref[...].reshape(nb * s, MLA_OUT)
    od = od_ref[...].reshape(nb * s, DIFF_W)
    y = _dot(om, wo_ref[:MLA_OUT, :]) + _dot(od, wo_ref[MLA_OUT:, :])
    _residual_and_route(x_ref[...], y, gate1_ref[...], g2_ref[...], sh2_ref[...], sc2_ref[...],
                        rw_ref[...], rb_ref[...], x1_ref, h2_ref, gates_ref)


def _route_outputs(nb_tot, s_tot, d, tok, tm, gate_index):
    return ([jax.ShapeDtypeStruct((nb_tot, s_tot, d), F32),
             jax.ShapeDtypeStruct((nb_tot, s_tot, d), BF16),
             jax.ShapeDtypeStruct((N_EXPERTS, nb_tot * s_tot), F32)],
            [tok(d), tok(d), pl.BlockSpec((N_EXPERTS, tm), gate_index)])


def _attn_merge(x3, om, od, w_o, gate1, g2, sh2, sc2, rw, rb, nb, s):
    nb_tot, s_tot, d = x3.shape
    tok = lambda w: pl.BlockSpec((nb, s, w), lambda i, j: (i, j, 0))
    mod = pl.BlockSpec((nb, 1, d), lambda i, j: (i, 0, 0))
    full = lambda a: pl.BlockSpec(a.shape, lambda i, j: (0,) * a.ndim)
    nj = s_tot // s
    out_shape, out_specs = _route_outputs(nb_tot, s_tot, d, tok, nb * s, lambda i, j: (0, i * nj + j))
    return pl.pallas_call(
        _merge_kernel,
        grid=(nb_tot // nb, nj),
        in_specs=[tok(d), tok(MLA_OUT), tok(DIFF_W), full(w_o), mod, full(g2), mod, mod, full(rw), full(rb)],
        out_specs=out_specs,
        out_shape=out_shape,
        compiler_params=_cparams("arbitrary", "arbitrary"),
        name="attn_merge",
    )(x3, om, od, w_o, gate1, g2, sh2, sc2, rw, rb)


CONV_TAIL = 32
SUBLANES = 8
CONV_ROWS = 32


def _conv_post(y, cn_ref, pw2_ref, pw2b_ref):
    y = _rms(y, y.shape[-1]) * cn_ref[...]
    y = y * jax.nn.sigmoid(y)
    return _dot(y, pw2_ref[...]) + pw2b_ref[...]


def _glu(h, pw1_ref, pw1b_ref):
    z = _dot(h, pw1_ref[...]) + pw1b_ref[...]
    c = z.shape[-1] // 2
    return z[:, :c] * jax.nn.sigmoid(z[:, c:])


def _conv_prompt_kernel(x_ref, sh_ref, sc_ref, g1_ref, pw1_ref, pw1b_ref, dw_ref, dwb_ref, cn_ref,
                        pw2_ref, pw2b_ref, gate1_ref, g2_ref, sh2_ref, sc2_ref, rw_ref, rb_ref,
                        x1_ref, h2_ref, gates_ref, st_ref, ubuf, shifted, ybuf):
    j = pl.program_id(1)
    _, s, d = x_ref.shape
    off = CONV_TAIL - (CONV_W - 1)
    span = CONV_TAIL + s - SUBLANES

    @pl.when(j == 0)
    def _():
        ubuf[0:CONV_TAIL, :] = jnp.zeros((CONV_TAIL, d), F32)

    @pl.when(j > 0)
    def _():
        ubuf[0:CONV_TAIL, :] = ubuf[s:s + CONV_TAIL, :]

    h = _modulate(x_ref[...], g1_ref[...], sh_ref[...], sc_ref[...])
    ubuf[CONV_TAIL:CONV_TAIL + s, :] = _glu(h, pw1_ref, pw1b_ref)
    st_ref[0, 0] = ubuf[s:s + CONV_TAIL, :]
    for b in range(SUBLANES):
        n_b = CONV_TAIL + s if b == 0 else span
        shifted[b, 0:n_b, :] = ubuf[b:b + n_b, :]

    def rows_body(c, carry):
        r0 = pl.multiple_of(c * CONV_ROWS, CONV_ROWS)
        acc = jnp.zeros((CONV_ROWS, d), F32) + dwb_ref[...]
        for w in range(CONV_W):
            a, b = divmod(off + w, SUBLANES)
            acc = acc + shifted[b, pl.ds(r0 + a * SUBLANES, CONV_ROWS), :] * dw_ref[w:w + 1, :]
        ybuf[pl.ds(r0, CONV_ROWS), :] = acc
        return carry

    lax.fori_loop(0, s // CONV_ROWS, rows_body, 0)
    y = _conv_post(ybuf[...], cn_ref, pw2_ref, pw2b_ref)
    _residual_and_route(x_ref[...], y, gate1_ref[...], g2_ref[...], sh2_ref[...], sc2_ref[...],
                        rw_ref[...], rb_ref[...], x1_ref, h2_ref, gates_ref)


def _conv_prompt(x3, shift, scale, g1, cw, gate1, g2, sh2, sc2, rw, rb, s):
    n, t, d = x3.shape
    nt = t // s
    tok = lambda w: pl.BlockSpec((1, s, w), lambda i, j: (i, j, 0))
    mod = pl.BlockSpec((1, 1, d), lambda i, j: (i, 0, 0))
    full = lambda a: pl.BlockSpec(a.shape, lambda i, j: (0,) * a.ndim)
    out_shape, out_specs = _route_outputs(n, t, d, tok, s, lambda i, j: (0, i * nt + j))
    out_shape.append(jax.ShapeDtypeStruct((n, nt, CONV_TAIL, d), F32))
    out_specs.append(pl.BlockSpec((1, 1, CONV_TAIL, d), lambda i, j: (i, j, 0, 0)))
    cargs = (cw["pw1"], cw["pw1_b"], cw["dw"], cw["dw_b"], cw["norm"], cw["pw2"], cw["pw2_b"])
    x1, h2, gates, tails = pl.pallas_call(
        _conv_prompt_kernel,
        grid=(n, nt),
        in_specs=[tok(d), mod, mod, full(g1)] + [full(a) for a in cargs]
                 + [mod, full(g2), mod, mod, full(rw), full(rb)],
        out_specs=out_specs,
        out_shape=out_shape,
        scratch_shapes=[pltpu.VMEM((CONV_TAIL + s, d), F32),
                        pltpu.VMEM((SUBLANES, CONV_TAIL + s, d), F32),
                        pltpu.VMEM((s, d), F32)],
        compiler_params=_cparams("arbitrary", "arbitrary"),
        name="conv_prompt",
    )(x3, shift, scale, g1, *cargs, gate1, g2, sh2, sc2, rw, rb)
    return x1, h2, gates, tails[:, nt - 1, CONV_TAIL - (CONV_W - 1):, :]


def _conv_sample_kernel(x_ref, st_ref, sh_ref, sc_ref, g1_ref, pw1_ref, pw1b_ref, dw_ref, dwb_ref, cn_ref,
                        pw2_ref, pw2b_ref, gate1_ref, g2_ref, sh2_ref, sc2_ref, rw_ref, rb_ref,
                        x1_ref, h2_ref, gates_ref, nst_ref, full_buf):
    s, nb, d = x_ref.shape
    hist = CONV_W - 1
    x = x_ref[...]
    y = _rms(x, d) * g1_ref[...][None]
    h = (y * (1.0 + sc_ref[...]) + sh_ref[...]).reshape(s * nb, d)
    full_buf[0:hist] = st_ref[...]
    full_buf[hist:hist + s] = _glu(h, pw1_ref, pw1b_ref).reshape(s, nb, d)
    acc = jnp.zeros((s, nb, d), F32) + dwb_ref[...][None]
    for w in range(CONV_W):
        acc = acc + full_buf[w:w + s] * dw_ref[w:w + 1, :][None]
    nst_ref[...] = full_buf[s:s + hist]
    yc = _conv_post(acc.reshape(s * nb, d), cn_ref, pw2_ref, pw2b_ref).reshape(s, nb, d)
    x1 = x + gate1_ref[...] * yc
    x1_ref[...] = x1
    y2 = _rms(x1, d) * g2_ref[...][None]
    h2 = (y2 * (1.0 + sc2_ref[...]) + sh2_ref[...]).reshape(s * nb, d)
    h2_ref[...] = h2.astype(BF16).reshape(s, nb, d)
    gates_ref[...] = _router_gates_t(h2, rw_ref[...], rb_ref[...])


def _conv_sample(x_t, state_t, shift, scale, g1, cw, gate1, g2, sh2, sc2, rw, rb, nb):
    s, nseq, d = x_t.shape
    hist = CONV_W - 1
    tok = lambda w: pl.BlockSpec((s, nb, w), lambda i: (0, i, 0))
    mod = pl.BlockSpec((1, nb, d), lambda i: (0, i, 0))
    st = pl.BlockSpec((hist, nb, d), lambda i: (0, i, 0))
    full = lambda a: pl.BlockSpec(a.shape, lambda i: (0,) * a.ndim)
    cargs = (cw["pw1"], cw["pw1_b"], cw["dw"], cw["dw_b"], cw["norm"], cw["pw2"], cw["pw2_b"])
    return pl.pallas_call(
        _conv_sample_kernel,
        grid=(nseq // nb,),
        in_specs=[tok(d), st, mod, mod, full(g1)] + [full(a) for a in cargs]
                 + [mod, full(g2), mod, mod, full(rw), full(rb)],
        out_specs=[tok(d), tok(d), pl.BlockSpec((N_EXPERTS, s * nb), lambda i: (0, i)), st],
        out_shape=[jax.ShapeDtypeStruct((s, nseq, d), F32),
                   jax.ShapeDtypeStruct((s, nseq, d), BF16),
                   jax.ShapeDtypeStruct((N_EXPERTS, s * nseq), F32),
                   jax.ShapeDtypeStruct((hist, nseq, d), F32)],
        scratch_shapes=[pltpu.VMEM((hist + s, nb, d), F32)],
        compiler_params=_cparams("arbitrary"),
        name="conv_sample",
    )(x_t, state_t, shift, scale, g1, *cargs, gate1, g2, sh2, sc2, rw, rb)


def _moe_kernel(h_ref, gates_ref, x1_ref, gate2_ref, wg_ref, wu_ref, wd_ref, o_ref, acc_ref):
    e = pl.program_id(2)
    ne = pl.num_programs(2)
    nb, s, d = h_ref.shape
    tm = nb * s

    @pl.when(e == 0)
    def _():
        acc_ref[...] = jnp.zeros_like(acc_ref)

    h = h_ref[...].reshape(tm, d)
    a = _dot(h, wg_ref[0])
    u = (a * jax.nn.sigmoid(a)) * _dot(h, wu_ref[0])
    gates = gates_ref[...].reshape(tm, N_EXPERTS)
    ge = jnp.sum(jnp.where(_lane_iota(gates.shape) == e, gates, 0.0), axis=-1, keepdims=True)
    acc_ref[...] += ge * _dot(u, wd_ref[0])

    @pl.when(e == ne - 1)
    def _():
        o_ref[...] = x1_ref[...] + gate2_ref[...] * acc_ref[...].reshape(nb, s, d)


def _moe(h2, gates, x1, gate2, wg, wu, wd, nb, s):
    nb_tot, s_tot, d = h2.shape
    ne, _, de = wg.shape
    tok = lambda w: pl.BlockSpec((nb, s, w), lambda i, j, e: (i, j, 0))
    mod = pl.BlockSpec((nb, 1, d), lambda i, j, e: (i, 0, 0))
    return pl.pallas_call(
        _moe_kernel,
        grid=(nb_tot // nb, s_tot // s, ne),
        in_specs=[tok(d), tok(N_EXPERTS), tok(d), mod,
                  pl.BlockSpec((1, d, de), lambda i, j, e: (e, 0, 0)),
                  pl.BlockSpec((1, d, de), lambda i, j, e: (e, 0, 0)),
                  pl.BlockSpec((1, de, d), lambda i, j, e: (e, 0, 0))],
        out_specs=tok(d),
        out_shape=jax.ShapeDtypeStruct((nb_tot, s_tot, d), F32),
        scratch_shapes=[pltpu.VMEM((nb * s, d), F32)],
        compiler_params=_cparams("arbitrary", "arbitrary", "arbitrary"),
        name="moe",
    )(h2, gates, x1, gate2, wg, wu, wd)


def _tied(g):
    return jnp.concatenate([g, g[QK_NOPE:], jnp.zeros((HEAD_PAD - QK_DIM,), F32)])[None]


def _rope_tables(pos):
    half = QK_ROPE // 2
    inv = ROPE_BASE ** (-jnp.arange(0, QK_ROPE, 2, dtype=F32) / QK_ROPE)
    ang = pos.astype(F32)[:, None] * inv[None, :]
    cos, sin = jnp.cos(ang), jnp.sin(ang)
    t = pos.shape[0]
    z = lambda n: jnp.zeros((t, n), F32)
    one = jnp.ones((t, QK_NOPE), F32)
    q_cos = jnp.concatenate([one, cos, cos, z(HEAD_PAD - QK_DIM)], axis=1)
    q_sa = jnp.concatenate([z(QK_NOPE), -sin, z(half), z(HEAD_PAD - QK_DIM)], axis=1)
    q_sb = jnp.concatenate([z(QK_NOPE), z(half), sin, z(HEAD_PAD - QK_DIM)], axis=1)
    k_cos = jnp.concatenate([cos, cos, z(LANES - QK_ROPE)], axis=1)
    k_sa = jnp.concatenate([-sin, z(LANES - half)], axis=1)
    k_sb = jnp.concatenate([z(half), sin, z(LANES - QK_ROPE)], axis=1)
    return jnp.stack([q_cos, q_sa, q_sb, k_cos, k_sa, k_sb])


def _attn_weights(w_in, q_a_norm, w_q_b, kv_a_norm, w_kv_b, mla_q_gain, mla_k_gain, diff_q_gain, diff_k_gain):
    d = w_in.shape[0]
    c1 = Q_LORA + KV_LORA
    w_in_p = jnp.concatenate([w_in[:, :c1], w_in[:, c1:c1 + QK_ROPE], jnp.zeros((d, LANES - QK_ROPE), F32),
                              w_in[:, c1 + QK_ROPE:]], axis=1).astype(BF16)
    wqb = jnp.pad(w_q_b.reshape(Q_LORA, MLA_HEADS, QK_DIM), ((0, 0), (0, 0), (0, HEAD_PAD - QK_DIM)))
    wkv = w_kv_b.reshape(KV_LORA, MLA_HEADS, QK_NOPE + V_HEAD)
    wkn = jnp.pad(wkv[:, :, :QK_NOPE], ((0, 0), (0, 0), (0, HEAD_PAD - QK_NOPE)))
    wv = wkv[:, :, QK_NOPE:].reshape(KV_LORA, MLA_OUT)
    return {
        "w_in": w_in_p,
        "q_a_norm": q_a_norm[None],
        "w_q_b": wqb.reshape(Q_LORA, MLA_HEADS * HEAD_PAD).astype(BF16),
        "kv_a_norm": kv_a_norm[None],
        "w_kv": jnp.concatenate([wkn.reshape(KV_LORA, MLA_HEADS * HEAD_PAD), wv], axis=1).astype(BF16),
        "w_kb": wkv[:, :, :QK_NOPE].reshape(KV_LORA, MLA_HEADS * QK_NOPE).astype(BF16),
        "w_v": wv.astype(BF16),
        "q_gain": _tied(mla_q_gain),
        "k_gain": _tied(mla_k_gain),
        "dq_gain": jnp.concatenate([diff_q_gain, diff_q_gain])[None],
        "dk_gain": jnp.concatenate([diff_k_gain, diff_k_gain])[None],
    }


PROMPT_TILE = 512
SAMPLE_SEQS = 64


def kernel(x_prompt, x_sample, cache_mla_latent, cache_mla_kpe, cache_diff_k, cache_diff_v, state_conv, page_table, c_prompt, c_sample, norm1, norm2, ada_w, ada_b, w_in, q_a_norm, w_q_b, kv_a_norm, w_kv_b, mla_q_gain, mla_k_gain, diff_q_gain, diff_k_gain, diff_lambda, diff_subln, w_o, conv_pw1, conv_pw1_b, conv_dw, conv_dw_b, conv_norm, conv_pw2, conv_pw2_b, router_w, router_bias, moe_w_gate, moe_w_up, moe_w_down):
    n, t, d = x_prompt.shape
    nseq, n_new, _ = x_sample.shape
    n_pages = page_table.shape[1]
    depth = norm1.shape[0]
    sp = min(PROMPT_TILE, t)
    sb = min(SAMPLE_SEQS, nseq)

    rows = -(-(n + nseq) // 8) * 8
    c_all = jnp.concatenate([c_prompt, c_sample, jnp.zeros((rows - n - nseq, d), F32)], axis=0)
    ada = _adaln(c_all, ada_w, ada_b).reshape(depth, rows, 6, d)

    router_w, rb = _router_layout(router_w, router_bias)
    xp, xs = x_prompt, x_sample
    outs = {k: [] for k in ("lat_p", "lat_s", "kpe_p", "kpe_s", "dk_p", "dk_s", "dv_p", "dv_s", "cv_p", "cv_s")}
    for l in range(depth):
        mp = [ada[l, :n, i][:, None, :] for i in range(6)]
        ms = [ada[l, n:n + nseq, i][:, None, :] for i in range(6)]
        g1, g2 = norm1[l][None], norm2[l][None]
        wg, wu, wd = moe_w_gate[l], moe_w_up[l], moe_w_down[l]
        if l % 2 == 0:
            a = l // 2
            lam_init = 0.8 - 0.6 * math.exp(-0.3 * l)
            wts = _attn_weights(w_in[a], q_a_norm[a], w_q_b[a], kv_a_norm[a], w_kv_b[a], mla_q_gain[a],
                                mla_k_gain[a], diff_q_gain[a], diff_k_gain[a])
            subln = diff_subln[a][None]
            w_o_b = w_o[a].astype(BF16)
            tab_p = _rope_tables(jnp.arange(t))
            q, ckv, kpe, dq, dk, dv, k, v, dkb, dvb = _attn_project(True, xp, mp[0], mp[1], g1, wts, tab_p, 1, sp)
            om = _mla_prompt_attention(q, k, v)
            od = _diff_prompt_attention(dq, dkb, dvb, diff_lambda[a], subln, lam_init)
            xp1, hp2, gp = _attn_merge(xp, om, od, w_o_b, mp[2], g2, mp[3], mp[4], router_w, rb, 1, sp)
            outs["lat_p"].append(ckv); outs["kpe_p"].append(kpe)
            outs["dk_p"].append(dk.reshape(n, t, DIFF_HEADS, 2 * DIFF_HD))
            outs["dv_p"].append(dv.reshape(n, t, DIFF_HEADS, 2 * DIFF_HD))
            tab_s = _rope_tables(n_pages * PAGE_SIZE + jnp.arange(n_new))
            q, ckv, kpe, dq, dk, dv = _attn_project(False, xs, ms[0], ms[1], g1, wts, tab_s, sb, n_new)
            qrow = q.reshape(nseq, n_new, MLA_HEADS, HEAD_PAD).transpose(0, 2, 1, 3).reshape(
                nseq, MLA_HEADS * n_new, HEAD_PAD)
            dqrow = dq.reshape(nseq, n_new, 2 * DIFF_HEADS, DIFF_HD).transpose(0, 2, 1, 3).reshape(
                nseq, 2 * DIFF_HEADS * n_new, DIFF_HD)
            hp_shape = cache_diff_k.shape[:2] + (PAGE_SIZE * DIFF_HEADS, 2 * DIFF_HD)
            om, od = _sample_attention(
                a, page_table, qrow, dqrow, ckv, kpe.transpose(0, 2, 1), dk, dv,
                cache_mla_latent, jnp.swapaxes(cache_mla_kpe, 2, 3),
                cache_diff_k.reshape(hp_shape), cache_diff_v.reshape(hp_shape),
                wts["w_kb"], wts["w_v"], wts["k_gain"], diff_lambda[a], subln, lam_init)
            xs1, hs2, gs = _attn_merge(xs, om, od, w_o_b, ms[2], g2, ms[3], ms[4], router_w, rb, sb, n_new)
            outs["lat_s"].append(ckv); outs["kpe_s"].append(kpe)
            outs["dk_s"].append(dk.reshape(nseq, n_new, DIFF_HEADS, 2 * DIFF_HD))
            outs["dv_s"].append(dv.reshape(nseq, n_new, DIFF_HEADS, 2 * DIFF_HD))
        else:
            c = l // 2
            cw = {"pw1": conv_pw1[c].astype(BF16), "pw1_b": conv_pw1_b[c][None], "dw": conv_dw[c],
                  "dw_b": conv_dw_b[c][None], "norm": conv_norm[c][None], "pw2": conv_pw2[c].astype(BF16),
                  "pw2_b": conv_pw2_b[c][None]}
            xp1, hp2, gp, st_p = _conv_prompt(xp, mp[0], mp[1], g1, cw, mp[2], g2, mp[3], mp[4], router_w, rb,
                                              min(256, t))
            outs["cv_p"].append(st_p)
            tm_ = lambda m: m.transpose(1, 0, 2)
            xs1_t, hs2_t, gs_t, st_s = _conv_sample(
                xs.transpose(1, 0, 2), state_conv[c].transpose(1, 0, 2), tm_(ms[0]), tm_(ms[1]), g1, cw,
                tm_(ms[2]), g2, tm_(ms[3]), tm_(ms[4]), router_w, rb, min(32, nseq))
            xs1, hs2 = xs1_t.transpose(1, 0, 2), hs2_t.transpose(1, 0, 2)
            nbc = min(32, nseq)
            gs = gs_t.reshape(N_EXPERTS, nseq // nbc, n_new, nbc).transpose(0, 1, 3, 2).reshape(N_EXPERTS, -1)
            outs["cv_s"].append(st_s.transpose(1, 0, 2))
        xp = _moe(hp2, _gates_dense(gp).reshape(n, t, N_EXPERTS), xp1, mp[5], wg, wu, wd, 1, min(1024, t))
        xs = _moe(hs2, _gates_dense(gs).reshape(nseq, n_new, N_EXPERTS), xs1, ms[5], wg, wu, wd, nseq, n_new)
    st = lambda k: jnp.stack(outs[k])
    return (xp, xs, st("lat_p"), st("lat_s"), st("kpe_p"), st("kpe_s"), st("dk_p"), st("dk_s"),
            st("dv_p"), st("dv_s"), st("cv_p"), st("cv_s"))
```

```python
import functools
import math

import jax
import jax.numpy as jnp
from jax import lax
from jax.experimental import pallas as pl
from jax.experimental.pallas import tpu as pltpu

F32 = jnp.float32
BF16 = jnp.bfloat16

D_MODEL = 1024
PAGE_SIZE = 128
MLA_HEADS = 8
QK_NOPE = 64
QK_ROPE = 32
QK_DIM = QK_NOPE + QK_ROPE
V_HEAD = 64
Q_LORA = 384
KV_LORA = 256
ROPE_BASE = 10000.0
MLA_OUT = MLA_HEADS * V_HEAD
DIFF_HEADS = 4
DIFF_HD = 64
DIFF_W = DIFF_HEADS * 2 * DIFF_HD
CONV_W = 31
N_EXPERTS = 16
N_GROUPS = 4
E_PER_GROUP = N_EXPERTS // N_GROUPS
D_EXPERT = 512
EPS = 1e-6
LOG2E = math.log2(math.e)

LANES = 128
HEAD_PAD = 128
VMEM_LIMIT = 48 * 1024 * 1024


def _cparams(*sem):
    return pltpu.CompilerParams(dimension_semantics=sem, vmem_limit_bytes=VMEM_LIMIT)


def _dot(a, b):
    return jnp.dot(a.astype(BF16), b.astype(BF16), preferred_element_type=F32)


def _dot_nt(a, b):
    return lax.dot_general(a.astype(BF16), b.astype(BF16), (((1,), (1,)), ((), ())),
                           preferred_element_type=F32)


def _dot_tn(a, b):
    return lax.dot_general(a.astype(BF16), b.astype(BF16), (((0,), (0,)), ((), ())),
                           preferred_element_type=F32)


def _rms(x, n):
    return x * lax.rsqrt(jnp.sum(x * x, axis=-1, keepdims=True) * (1.0 / n) + EPS)


def _lane_iota(shape):
    return lax.broadcasted_iota(jnp.int32, shape, len(shape) - 1)


def _adaln_kernel(c_ref, w_ref, b_ref, o_ref):
    c = c_ref[...]
    s = c * jax.nn.sigmoid(c)
    o_ref[0] = _dot(s, w_ref[0]) + b_ref[0]


def _adaln(c_all, ada_w, ada_b):
    n_layers, d, d6 = ada_w.shape
    rows = c_all.shape[0]
    tn = 1536
    return pl.pallas_call(
        _adaln_kernel,
        grid=(n_layers, d6 // tn),
        in_specs=[pl.BlockSpec((rows, d), lambda l, j: (0, 0)),
                  pl.BlockSpec((1, d, tn), lambda l, j: (l, 0, j)),
                  pl.BlockSpec((1, 1, tn), lambda l, j: (l, 0, j))],
        out_specs=pl.BlockSpec((1, rows, tn), lambda l, j: (l, 0, j)),
        out_shape=jax.ShapeDtypeStruct((n_layers, rows, d6), F32),
        compiler_params=_cparams("arbitrary", "arbitrary"),
        name="adaln",
    )(c_all, ada_w, ada_b.reshape(n_layers, 1, d6))


def _modulate(x3, g, shift, scale):
    nb, s, d = x3.shape
    y = _rms(x3, d) * g[None]
    y = y * (1.0 + scale) + shift
    return y.reshape(nb * s, d)


def _bcast_rows(tab, nb):
    s, l = tab.shape
    return jnp.broadcast_to(tab[None], (nb, s, l)).reshape(nb * s, l)


def _rope_block(x, cos_t, sin_a, sin_b, half):
    n = x.shape[-1]
    return x * cos_t + pltpu.roll(x, n - half, 1) * sin_a + pltpu.roll(x, half, 1) * sin_b


def _router_gates_t(h2, rwt, rbt):
    tm = h2.shape[0]
    h_hi = h2.astype(BF16)
    h_lo = (h2 - h_hi.astype(F32)).astype(BF16)
    by_hi = _dot_nt(rwt, h_hi)
    logits = by_hi[:N_EXPERTS] + by_hi[N_EXPERTS:] + _dot_nt(rwt[:N_EXPERTS], h_lo)
    scores = jax.nn.sigmoid(logits)
    sel = scores + rbt
    a = [sel[i * N_GROUPS:(i + 1) * N_GROUPS] for i in range(E_PER_GROUP)]
    sc = [scores[i * N_GROUPS:(i + 1) * N_GROUPS] for i in range(E_PER_GROUP)]
    member = []
    for i in range(E_PER_GROUP):
        rank = jnp.zeros((N_GROUPS, tm), F32)
        for j in range(E_PER_GROUP):
            if j != i:
                ahead = (a[j] >= a[i]) if j < i else (a[j] > a[i])
                rank = rank + jnp.where(ahead, 1.0, 0.0)
        member.append(rank < 2.0)
    gscore = jnp.zeros((N_GROUPS, tm), F32)
    for i in range(E_PER_GROUP):
        gscore = gscore + jnp.where(member[i], a[i], 0.0)
    best = gscore[0:1]
    best_g = jnp.zeros((1, tm), jnp.int32)
    for g in range(1, N_GROUPS):
        better = gscore[g:g + 1] > best
        best = jnp.where(better, gscore[g:g + 1], best)
        best_g = jnp.where(better, g, best_g)
    in_group = lax.broadcasted_iota(jnp.int32, (N_GROUPS, tm), 0) == best_g
    picked = [jnp.where(in_group, jnp.where(member[i], sc[i], 0.0), 0.0) for i in range(E_PER_GROUP)]
    total = picked[0]
    for i in range(1, E_PER_GROUP):
        total = total + picked[i]
    inv = 1.0 / jnp.sum(total, axis=0, keepdims=True)
    return jnp.concatenate([p * inv for p in picked], axis=0)


def _residual_and_route(x3, y, gate1, g2, sh2, sc2, rwt, rbt, x1_ref, h2_ref, gates_ref):
    nb, s, d = x3.shape
    x1 = x3 + gate1 * y.reshape(nb, s, d)
    x1_ref[...] = x1
    h2 = _modulate(x1, g2, sh2, sc2)
    h2_ref[...] = h2.astype(BF16).reshape(nb, s, d)
    gates_ref[...] = _router_gates_t(h2, rwt, rbt)


def _gates_dense(gates_t):
    m = gates_t.shape[1]
    return gates_t.reshape(E_PER_GROUP, N_GROUPS, m).transpose(2, 1, 0).reshape(m, N_EXPERTS)


def _router_layout(router_w, router_bias):
    d = router_w.shape[0]
    rwt = router_w.T.reshape(N_GROUPS, E_PER_GROUP, d).transpose(1, 0, 2).reshape(N_EXPERTS, d)
    rbt = router_bias.reshape(N_GROUPS, E_PER_GROUP).T.reshape(N_EXPERTS, 1)
    hi = rwt.astype(BF16)
    lo = (rwt - hi.astype(F32)).astype(BF16)
    return jnp.concatenate([hi, lo], axis=0), rbt


IN_PAD = Q_LORA + KV_LORA + LANES + 3 * DIFF_W
C_QA, C_CKV, C_KPE, C_DQ, C_DK, C_DV = 0, Q_LORA, Q_LORA + KV_LORA, 768, 1280, 1792


def _pair_norm(x, gain2):
    outs = []
    for h in range(DIFF_HEADS):
        blk = x[:, h * LANES:(h + 1) * LANES]
        lo = _lane_iota(blk.shape) < DIFF_HD
        sq = blk * blk
        s_lo = jnp.sum(jnp.where(lo, sq, 0.0), axis=-1, keepdims=True)
        s_all = jnp.sum(sq, axis=-1, keepdims=True)
        r_lo = lax.rsqrt(s_lo * (1.0 / DIFF_HD) + EPS)
        r_hi = lax.rsqrt((s_all - s_lo) * (1.0 / DIFF_HD) + EPS)
        outs.append(blk * jnp.where(lo, r_lo, r_hi) * gain2)
    return jnp.concatenate(outs, axis=1)


def _proj_kernel(prompt, x_ref, sh_ref, sc_ref, g1_ref, win_ref, qan_ref, wqb_ref, kvn_ref, wkv_ref,
                 qg_ref, kg_ref, dqg_ref, dkg_ref, tab_ref, *outs):
    if prompt:
        q_ref, ckv_ref, kpe_ref, dq_ref, dk_ref, dv_ref, k_ref, v_ref, dkb_ref, dvb_ref = outs
    else:
        q_ref, ckv_ref, kpe_ref, dq_ref, dk_ref, dv_ref = outs
    nb, s, _ = x_ref.shape
    tm = nb * s
    h = _modulate(x_ref[...], g1_ref[...], sh_ref[...], sc_ref[...])
    z = _dot(h, win_ref[...])
    tab = tab_ref[...]
    qcos, qsa, qsb, kcos, ksa, ksb = [_bcast_rows(tab[i], nb) for i in range(6)]

    ckv = _rms(z[:, C_CKV:C_CKV + KV_LORA], KV_LORA) * kvn_ref[...]
    ckv_ref[...] = ckv.reshape(nb, s, KV_LORA)
    kpe = _rope_block(z[:, C_KPE:C_KPE + LANES], kcos, ksa, ksb, QK_ROPE // 2)
    kpe_ref[...] = kpe[:, :QK_ROPE].reshape(nb, s, QK_ROPE)

    qa = _rms(z[:, C_QA:C_QA + Q_LORA], Q_LORA) * qan_ref[...]
    qh = _dot(qa, wqb_ref[...])
    q_scale = QK_DIM ** -0.5 * LOG2E if prompt else 1.0
    q_out = []
    for hd in range(MLA_HEADS):
        blk = qh[:, hd * HEAD_PAD:(hd + 1) * HEAD_PAD]
        blk = _rope_block(blk, qcos, qsa, qsb, QK_ROPE // 2)
        q_out.append(_rms(blk, QK_DIM) * (qg_ref[...] * q_scale))
    q_ref[...] = jnp.concatenate(q_out, axis=1).astype(q_ref.dtype).reshape(nb, s, MLA_HEADS * HEAD_PAD)

    d_scale = DIFF_HD ** -0.5 * LOG2E if prompt else 1.0
    dq = _pair_norm(z[:, C_DQ:C_DQ + DIFF_W], dqg_ref[...] * d_scale)
    dq_ref[...] = dq.astype(dq_ref.dtype).reshape(nb, s, DIFF_W)
    dk = _pair_norm(z[:, C_DK:C_DK + DIFF_W], dkg_ref[...])
    dk_ref[...] = dk.reshape(nb, s, DIFF_W)
    dv = z[:, C_DV:C_DV + DIFF_W]
    dv_ref[...] = dv.reshape(nb, s, DIFF_W)

    if prompt:
        dkb_ref[...] = dk.astype(BF16).reshape(nb, s, DIFF_W)
        dvb_ref[...] = dv.astype(BF16).reshape(nb, s, DIFF_W)
        kv = _dot(ckv, wkv_ref[...])
        kpe_sh = pltpu.roll(kpe, QK_NOPE, 1)
        k_out = []
        for hd in range(MLA_HEADS):
            blk = kv[:, hd * HEAD_PAD:(hd + 1) * HEAD_PAD] + kpe_sh
            k_out.append(_rms(blk, QK_DIM) * kg_ref[...])
        k_ref[...] = jnp.concatenate(k_out, axis=1).astype(BF16).reshape(nb, s, MLA_HEADS * HEAD_PAD)
        v_ref[...] = kv[:, MLA_HEADS * HEAD_PAD:].astype(BF16).reshape(nb, s, MLA_OUT)


def _attn_project(prompt, x3, shift, scale, g1, wts, tab, nb, s):
    nb_tot, s_tot, d = x3.shape
    grid = (nb_tot // nb, s_tot // s)
    tok = lambda w: pl.BlockSpec((nb, s, w), lambda i, j: (i, j, 0))
    mod = pl.BlockSpec((nb, 1, d), lambda i, j: (i, 0, 0))
    full = lambda a: pl.BlockSpec(a.shape, lambda i, j: (0,) * a.ndim)
    qdt = BF16 if prompt else F32
    out_shape = [jax.ShapeDtypeStruct((nb_tot, s_tot, MLA_HEADS * HEAD_PAD), qdt),
                 jax.ShapeDtypeStruct((nb_tot, s_tot, KV_LORA), F32),
                 jax.ShapeDtypeStruct((nb_tot, s_tot, QK_ROPE), F32),
                 jax.ShapeDtypeStruct((nb_tot, s_tot, DIFF_W), qdt),
                 jax.ShapeDtypeStruct((nb_tot, s_tot, DIFF_W), F32),
                 jax.ShapeDtypeStruct((nb_tot, s_tot, DIFF_W), F32)]
    out_specs = [tok(MLA_HEADS * HEAD_PAD), tok(KV_LORA), tok(QK_ROPE), tok(DIFF_W), tok(DIFF_W), tok(DIFF_W)]
    if prompt:
        out_shape += [jax.ShapeDtypeStruct((nb_tot, s_tot, MLA_HEADS * HEAD_PAD), BF16),
                      jax.ShapeDtypeStruct((nb_tot, s_tot, MLA_OUT), BF16),
                      jax.ShapeDtypeStruct((nb_tot, s_tot, DIFF_W), BF16),
                      jax.ShapeDtypeStruct((nb_tot, s_tot, DIFF_W), BF16)]
        out_specs += [tok(MLA_HEADS * HEAD_PAD), tok(MLA_OUT), tok(DIFF_W), tok(DIFF_W)]
    args = (g1, wts["w_in"], wts["q_a_norm"], wts["w_q_b"], wts["kv_a_norm"], wts["w_kv"],
            wts["q_gain"], wts["k_gain"], wts["dq_gain"], wts["dk_gain"])
    return pl.pallas_call(
        functools.partial(_proj_kernel, prompt),
        grid=grid,
        in_specs=[tok(d), mod, mod] + [full(a) for a in args]
                 + [pl.BlockSpec((6, s, LANES), lambda i, j: (0, j, 0))],
        out_specs=out_specs,
        out_shape=out_shape,
        compiler_params=_cparams("arbitrary", "arbitrary"),
        name="attn_project_prompt" if prompt else "attn_project_sample",
    )(x3, shift, scale, *args, tab)


ATT_BLK = 1024
ATT_SUB = 512


def _flash_step(q, k, v, m, acc, mask=None):
    s = _dot_nt(q, k)
    if mask is not None:
        s = jnp.where(mask, s, -jnp.inf)
    m_new = jnp.maximum(m, jnp.max(s, axis=-1, keepdims=True))
    alpha = jnp.exp2(m - m_new)
    p = jnp.exp2(s - m_new)
    return m_new, alpha * acc + _dot(p, v)


def _causal_mask(n):
    return lax.broadcasted_iota(jnp.int32, (n, n), 1) <= lax.broadcasted_iota(jnp.int32, (n, n), 0)


def _flash_init(rows, width):
    return jnp.full((rows, 1), -jnp.inf, F32), jnp.zeros((rows, width), F32)


def _flash_diagonal(q, kv_rows, carry, blk, sub):
    m, acc = carry
    ms, accs = [], []
    tri = _causal_mask(sub)
    for r in range(blk // sub):
        rows = slice(r * sub, (r + 1) * sub)
        c_r = (m[rows], acc[rows])
        for c in range(r + 1):
            c_r = _flash_step(q[rows], *kv_rows(c), *c_r, mask=tri if c == r else None)
        ms.append(c_r[0])
        accs.append(c_r[1])
    return jnp.concatenate(ms, axis=0), jnp.concatenate(accs, axis=0)


def _mla_prompt_kernel(blk, sub, q_ref, k_ref, v_ref, o_ref):
    i = pl.program_id(2)
    heads = (0, 1)
    qs = [q_ref[0, :, hh * HEAD_PAD:(hh + 1) * HEAD_PAD] for hh in heads]

    def kv(start, size, hh):
        rows = pl.ds(pl.multiple_of(start, size), size)
        k = k_ref[0, rows, hh * HEAD_PAD:(hh + 1) * HEAD_PAD]
        vb = v_ref[0, rows, :]
        lane = _lane_iota(vb.shape)
        own = (lane < V_HEAD) if hh == 0 else (lane >= V_HEAD)
        one_at = V_HEAD if hh == 0 else 0
        v = jnp.where(own, vb, jnp.where(lane == one_at, 1.0, 0.0).astype(vb.dtype))
        return k, v

    def body(j, carry):
        return tuple(_flash_step(qs[hh], *kv(j * blk, blk, hh), *carry[hh]) for hh in heads)

    init = _flash_init(blk, 2 * V_HEAD)
    carry = lax.fori_loop(0, i, body, (init, init))
    accs = [_flash_diagonal(qs[hh], lambda c, hh=hh: kv(i * blk + c * sub, sub, hh), carry[hh], blk, sub)[1]
            for hh in heads]
    lane = _lane_iota(accs[0].shape)
    out = jnp.where(lane < V_HEAD, accs[0] / accs[0][:, V_HEAD:V_HEAD + 1], accs[1] / accs[1][:, 0:1])
    o_ref[0] = out.astype(o_ref.dtype)


def _mla_prompt_attention(q, k, v):
    n, t, _ = q.shape
    blk = min(ATT_BLK, t)
    sub = min(ATT_SUB, blk)
    return pl.pallas_call(
        functools.partial(_mla_prompt_kernel, blk, sub),
        grid=(n, MLA_HEADS // 2, t // blk),
        in_specs=[pl.BlockSpec((1, blk, 2 * HEAD_PAD), lambda b, h, i: (b, i, h)),
                  pl.BlockSpec((1, t, 2 * HEAD_PAD), lambda b, h, i: (b, 0, h)),
                  pl.BlockSpec((1, t, 2 * V_HEAD), lambda b, h, i: (b, 0, h))],
        out_specs=pl.BlockSpec((1, blk, 2 * V_HEAD), lambda b, h, i: (b, i, h)),
        out_shape=jax.ShapeDtypeStruct((n, t, MLA_OUT), BF16),
        compiler_params=_cparams("arbitrary", "arbitrary", "arbitrary"),
        name="mla_prompt_attention",
    )(q, k, v)


def _diff_lambda(lp_ref, lam_init):
    lp = lp_ref[...]
    a = jnp.sum(lp[0:1] * lp[1:2], axis=-1, keepdims=True)
    b = jnp.sum(lp[2:3] * lp[3:4], axis=-1, keepdims=True)
    return jnp.exp(a) - jnp.exp(b) + lam_init


ALIBI_SPLIT = 3


def _alibi_tables(t):
    kpos = jnp.arange(t)
    a, b = (kpos // 64).astype(F32), (kpos % 64).astype(F32)
    pad = jnp.zeros((t, LANES - 2 * ALIBI_SPLIT), F32)
    k_tab = jnp.concatenate([a[:, None]] * ALIBI_SPLIT + [b[:, None]] * ALIBI_SPLIT + [pad], axis=1).astype(BF16)
    pieces, rest = [], jnp.asarray(LOG2E, F32)
    for _ in range(ALIBI_SPLIT):
        p = rest.astype(BF16).astype(F32)
        pieces.append(p)
        rest = rest - p
    c = jnp.stack(pieces)
    slopes = 2.0 ** (-8.0 * jnp.arange(1, DIFF_HEADS + 1, dtype=F32) / DIFF_HEADS)
    q_tab = jnp.concatenate([64.0 * slopes[:, None] * c[None], slopes[:, None] * c[None],
                             jnp.zeros((DIFF_HEADS, LANES - 2 * ALIBI_SPLIT), F32)], axis=1)
    return k_tab, q_tab[:, None, :].astype(BF16)


def _diff_prompt_kernel(lam_init, blk, sub, q_ref, k_ref, v_ref, kt_ref, qt_ref, lp_ref, sub_ref, o_ref):
    i = pl.program_id(2)
    width = 2 * DIFF_HD
    q_all = q_ref[0]
    lo = _lane_iota(q_all.shape) < DIFF_HD
    q_pos = jnp.broadcast_to(qt_ref[0], q_all.shape)
    qs = (jnp.concatenate([jnp.where(lo, q_all, 0), q_pos], axis=1),
          jnp.concatenate([jnp.where(lo, 0, q_all), q_pos], axis=1))

    def kv(start, size):
        rows = pl.ds(pl.multiple_of(start, size), size)
        ones = jnp.where(_lane_iota((size, LANES)) == 0, 1.0, 0.0).astype(BF16)
        k = jnp.concatenate([k_ref[0, rows, :], kt_ref[rows, :]], axis=1)
        v = jnp.concatenate([v_ref[0, rows, :], ones], axis=1)
        return k, v

    def body(j, carry):
        k, v = kv(j * blk, blk)
        return tuple(_flash_step(qs[c], k, v, *carry[c]) for c in range(2))

    init = _flash_init(blk, width + LANES)
    carry = lax.fori_loop(0, i, body, (init, init))
    a0, a1 = [_flash_diagonal(qs[c], lambda r: kv(i * blk + r * sub, sub), carry[c], blk, sub)[1] for c in range(2)]
    lam = _diff_lambda(lp_ref, lam_init)
    od = a0[:, :width] / a0[:, width:width + 1] - lam * (a1[:, :width] / a1[:, width:width + 1])
    od = _rms(od, width) * sub_ref[...] * (1.0 - lam_init)
    o_ref[0] = od.astype(o_ref.dtype)


def _diff_prompt_attention(dq, dk, dv, diff_lambda, subln, lam_init):
    n, t, _ = dq.shape
    blkw = 2 * DIFF_HD
    blk = min(ATT_BLK, t)
    sub = min(ATT_SUB, blk)
    k_tab, q_tab = _alibi_tables(t)
    return pl.pallas_call(
        functools.partial(_diff_prompt_kernel, lam_init, blk, sub),
        grid=(n, DIFF_HEADS, t // blk),
        in_specs=[pl.BlockSpec((1, blk, blkw), lambda b, h, i: (b, i, h)),
                  pl.BlockSpec((1, t, blkw), lambda b, h, i: (b, 0, h)),
                  pl.BlockSpec((1, t, blkw), lambda b, h, i: (b, 0, h)),
                  pl.BlockSpec((t, LANES), lambda b, h, i: (0, 0)),
                  pl.BlockSpec((1, 1, LANES), lambda b, h, i: (h, 0, 0)),
                  pl.BlockSpec(diff_lambda.shape, lambda b, h, i: (0, 0)),
                  pl.BlockSpec(subln.shape, lambda b, h, i: (0, 0))],
        out_specs=pl.BlockSpec((1, blk, blkw), lambda b, h, i: (b, i, h)),
        out_shape=jax.ShapeDtypeStruct((n, t, DIFF_W), BF16),
        compiler_params=_cparams("arbitrary", "arbitrary", "arbitrary"),
        name="diff_prompt_attention",
    )(dq, dk, dv, k_tab, q_tab, diff_lambda, subln)


PAGES_PER_CHUNK = 16
N_SLOTS = 2


def _sample_attn_kernel(lam_init, layer, n_chunks, pt_ref,
                        qrow_ref, dqrow_ref, nckv_ref, nkpe_ref, ndk_ref, ndv_ref,
                        lat_hbm, kpe_hbm, dk_hbm, dv_hbm,
                        wkb_ref, wv_ref, kg_ref, lp_ref, sub_ref,
                        om_ref, od_ref,
                        lat_buf, kpe_buf, dk_buf, dv_buf, sems):
    b = pl.program_id(0)
    nb = pl.num_programs(0)
    ppc = PAGES_PER_CHUNK
    n_new = nckv_ref.shape[1]
    rows = MLA_HEADS * n_new
    hrows = PAGE_SIZE * DIFF_HEADS

    def copies(seq, chunk, slot):
        out = []
        for p in range(ppc):
            page = pt_ref[seq, chunk * ppc + p]
            out.append(pltpu.make_async_copy(lat_hbm.at[layer, page], lat_buf.at[slot, p], sems.at[0, slot]))
            out.append(pltpu.make_async_copy(kpe_hbm.at[layer, page], kpe_buf.at[slot, p], sems.at[1, slot]))
            out.append(pltpu.make_async_copy(dk_hbm.at[layer, page], dk_buf.at[slot, pl.ds(p * hrows, hrows)],
                                             sems.at[2, slot]))
            out.append(pltpu.make_async_copy(dv_hbm.at[layer, page], dv_buf.at[slot, pl.ds(p * hrows, hrows)],
                                             sems.at[3, slot]))
        return out

    @pl.when(b == 0)
    def _():
        for c in copies(0, 0, 0):
            c.start()

    qrow = qrow_ref[0]
    qt = qrow * (kg_ref[...] * (QK_DIM ** -0.5))
    lane = _lane_iota(qt.shape)
    qn = jnp.where(lane < QK_NOPE, qt, 0.0)
    qn2 = qn + pltpu.roll(qn, QK_NOPE, 1)
    qbd = jnp.concatenate([qn2] * (MLA_HEADS * QK_NOPE // LANES), axis=1)
    col = _lane_iota(qbd.shape)
    row = lax.broadcasted_iota(jnp.int32, qbd.shape, 0)
    blockdiag = col // QK_NOPE == row // n_new
    qbd = jnp.where(blockdiag, qbd, 0.0)
    qlat = _dot_nt(qbd, wkb_ref[...]).astype(BF16)
    qrope = pltpu.roll(qt, LANES - QK_NOPE, 1)[:, :QK_ROPE].astype(BF16)
    ind = jnp.where(blockdiag, 1.0, 0.0).astype(BF16)

    dqrow = dqrow_ref[0] * (DIFF_HD ** -0.5)
    dq2 = jnp.concatenate([dqrow, dqrow], axis=1)
    dqbd = jnp.concatenate([dq2] * (DIFF_W // LANES), axis=1)
    dqbd = jnp.where(blockdiag, dqbd, 0.0).astype(BF16)
    hd_col = lax.broadcasted_iota(jnp.int32, (rows, 1), 0) // (2 * n_new) + 1
    slope_col = jnp.exp2(-8.0 * hd_col.astype(F32) / DIFF_HEADS)

    def softmax_step(s, m, l):
        m_new = jnp.maximum(m, jnp.max(s, axis=-1, keepdims=True))
        alpha = jnp.exp(m - m_new)
        p = jnp.exp(s - m_new)
        return p, alpha, m_new, alpha * l + jnp.sum(p, axis=-1, keepdims=True)

    def chunk_update(state, ckv, kpe_t, dk, dv, kpos, mask):
        (m_a, l_a, acc_a), (m_d, l_d, acc_d) = state
        ckv_b = ckv.astype(BF16)
        kn = jnp.dot(ckv_b, wkb_ref[...], preferred_element_type=F32)
        ss = _dot_nt(ind, kn * kn) + jnp.sum(kpe_t * kpe_t, axis=0, keepdims=True)
        r = lax.rsqrt(ss * (1.0 / QK_DIM) + EPS)
        s = (_dot_nt(qlat, ckv_b) + _dot(qrope, kpe_t)) * r
        if mask is not None:
            s = jnp.where(mask, s, -jnp.inf)
        p, alpha, m_a, l_a = softmax_step(s, m_a, l_a)
        acc_a = acc_a * alpha + _dot(p, ckv_b)

        sd = _dot_nt(dqbd, dk) + slope_col * kpos
        if mask is not None:
            sd = jnp.where(mask, sd, -jnp.inf)
        pd, alpha_d, m_d, l_d = softmax_step(sd, m_d, l_d)
        acc_d = acc_d * alpha_d + _dot(pd, dv)
        return (m_a, l_a, acc_a), (m_d, l_d, acc_d)

    def init(width):
        return (jnp.full((rows, 1), -jnp.inf, F32), jnp.zeros((rows, 1), F32), jnp.zeros((rows, width), F32))

    def heads_to_lanes(buf, slot, n):
        return jnp.concatenate([buf[slot, pl.ds(h, n, stride=DIFF_HEADS), :] for h in range(DIFF_HEADS)], axis=1)

    state = (init(KV_LORA), init(DIFF_W))
    kc = ppc * PAGE_SIZE
    kpos_c = lax.broadcasted_iota(jnp.int32, (1, kc), 1).astype(F32)
    for c in range(n_chunks):
        slot = c % N_SLOTS
        nxt = (c + 1) % N_SLOTS
        if c + 1 < n_chunks:
            for cp in copies(b, c + 1, nxt):
                cp.start()
        else:
            @pl.when(b + 1 < nb)
            def _():
                for cp in copies(b + 1, 0, nxt):
                    cp.start()
        for cp in copies(b, c, slot):
            cp.wait()
        kpe_t = jnp.concatenate([kpe_buf[slot, p] for p in range(ppc)], axis=1)
        state = chunk_update(
            state,
            lat_buf[slot].reshape(kc, KV_LORA), kpe_t,
            heads_to_lanes(dk_buf, slot, kc), heads_to_lanes(dv_buf, slot, kc),
            kpos_c + float(c * kc), None)

    past = n_chunks * kc
    key_t = lax.broadcasted_iota(jnp.int32, (rows, n_new), 1)
    qry_t = lax.broadcasted_iota(jnp.int32, (rows, n_new), 0) % n_new
    kpos_n = lax.broadcasted_iota(jnp.int32, (1, n_new), 1).astype(F32) + float(past)
    state = chunk_update(state, nckv_ref[0], nkpe_ref[0], ndk_ref[0], ndv_ref[0], kpos_n, key_t <= qry_t)

    (_, l_a, acc_a), (_, l_d, acc_d) = state
    ov = _dot(acc_a / l_a, wv_ref[...])
    om = jnp.concatenate([ov[h * n_new:(h + 1) * n_new, h * V_HEAD:(h + 1) * V_HEAD]
                          for h in range(MLA_HEADS)], axis=1)
    om_ref[0] = om.astype(om_ref.dtype)

    lam = _diff_lambda(lp_ref, lam_init)
    on = acc_d / l_d
    ods = []
    for h in range(DIFF_HEADS):
        r0 = (2 * h) * n_new
        o0 = on[r0:r0 + n_new, h * LANES:(h + 1) * LANES]
        o1 = on[r0 + n_new:r0 + 2 * n_new, h * LANES:(h + 1) * LANES]
        o = o0 - lam * o1
        ods.append(_rms(o, 2 * DIFF_HD) * sub_ref[...] * (1.0 - lam_init))
    od_ref[0] = jnp.concatenate(ods, axis=1).astype(od_ref.dtype)


def _sample_attention(layer, page_table, qrow, dqrow, nckv, nkpe, ndk, ndv, lat, kpe, dk, dv,
                      wkb, wv, k_gain, diff_lambda, subln, lam_init):
    nseq, n_pages = page_table.shape
    n_new = nckv.shape[1]
    n_chunks = n_pages // PAGES_PER_CHUNK
    ppc = PAGES_PER_CHUNK
    seq = lambda a: pl.BlockSpec((1,) + a.shape[1:], lambda b, pt: (b,) + (0,) * (a.ndim - 1))
    full = lambda a: pl.BlockSpec(a.shape, lambda b, pt: (0,) * a.ndim)
    hbm = pl.BlockSpec(memory_space=pl.ANY)
    grid_spec = pltpu.PrefetchScalarGridSpec(
        num_scalar_prefetch=1,
        grid=(nseq,),
        in_specs=[seq(qrow), seq(dqrow), seq(nckv), seq(nkpe), seq(ndk), seq(ndv), hbm, hbm, hbm, hbm,
                  full(wkb), full(wv), full(k_gain), full(diff_lambda), full(subln)],
        out_specs=[pl.BlockSpec((1, n_new, MLA_OUT), lambda b, pt: (b, 0, 0)),
                   pl.BlockSpec((1, n_new, DIFF_W), lambda b, pt: (b, 0, 0))],
        scratch_shapes=[pltpu.VMEM((N_SLOTS, ppc, PAGE_SIZE, KV_LORA), F32),
                        pltpu.VMEM((N_SLOTS, ppc, QK_ROPE, PAGE_SIZE), F32),
                        pltpu.VMEM((N_SLOTS, ppc * PAGE_SIZE * DIFF_HEADS, 2 * DIFF_HD), F32),
                        pltpu.VMEM((N_SLOTS, ppc * PAGE_SIZE * DIFF_HEADS, 2 * DIFF_HD), F32),
                        pltpu.SemaphoreType.DMA((4, N_SLOTS))],
    )
    return pl.pallas_call(
        functools.partial(_sample_attn_kernel, lam_init, layer, n_chunks),
        grid_spec=grid_spec,
        out_shape=[jax.ShapeDtypeStruct((nseq, n_new, MLA_OUT), BF16),
                   jax.ShapeDtypeStruct((nseq, n_new, DIFF_W), BF16)],
        compiler_params=_cparams("arbitrary"),
        name="sample_attention",
    )(page_table, qrow, dqrow, nckv, nkpe, ndk, ndv, lat, kpe, dk, dv, wkb, wv, k_gain, diff_lambda, subln)


def _merge_kernel(x_ref, om_ref, od_ref, wo_ref, gate1_ref, g2_ref, sh2_ref, sc2_ref, rw_ref, rb_ref,
                  x1_ref, h2_ref, gates_ref):
    nb, s, d = x_ref.shape
    om = om_ref[...].reshape(nb * s, MLA_OUT)
    od = od_ref[...].reshape(nb * s, DIFF_W)
    y = _dot(om, wo_ref[:MLA_OUT, :]) + _dot(od, wo_ref[MLA_OUT:, :])
    _residual_and_route(x_ref[...], y, gate1_ref[...], g2_ref[...], sh2_ref[...], sc2_ref[...],
                        rw_ref[...], rb_ref[...], x1_ref, h2_ref, gates_ref)


def _route_outputs(nb_tot, s_tot, d, tok, tm, gate_index):
    return ([jax.ShapeDtypeStruct((nb_tot, s_tot, d), F32),
             jax.ShapeDtypeStruct((nb_tot, s_tot, d), BF16),
             jax.ShapeDtypeStruct((N_EXPERTS, nb_tot * s_tot), F32)],
            [tok(d), tok(d), pl.BlockSpec((N_EXPERTS, tm), gate_index)])


def _attn_merge(x3, om, od, w_o, gate1, g2, sh2, sc2, rw, rb, nb, s):
    nb_tot, s_tot, d = x3.shape
    tok = lambda w: pl.BlockSpec((nb, s, w), lambda i, j: (i, j, 0))
    mod = pl.BlockSpec((nb, 1, d), lambda i, j: (i, 0, 0))
    full = lambda a: pl.BlockSpec(a.shape, lambda i, j: (0,) * a.ndim)
    nj = s_tot // s
    out_shape, out_specs = _route_outputs(nb_tot, s_tot, d, tok, nb * s, lambda i, j: (0, i * nj + j))
    return pl.pallas_call(
        _merge_kernel,
        grid=(nb_tot // nb, nj),
        in_specs=[tok(d), tok(MLA_OUT), tok(DIFF_W), full(w_o), mod, full(g2), mod, mod, full(rw), full(rb)],
        out_specs=out_specs,
        out_shape=out_shape,
        compiler_params=_cparams("arbitrary", "arbitrary"),
        name="attn_merge",
    )(x3, om, od, w_o, gate1, g2, sh2, sc2, rw, rb)


CONV_TAIL = 32
SUBLANES = 8
CONV_ROWS = 32


def _conv_post(y, cn_ref, pw2_ref, pw2b_ref):
    y = _rms(y, y.shape[-1]) * cn_ref[...]
    y = y * jax.nn.sigmoid(y)
    return _dot(y, pw2_ref[...]) + pw2b_ref[...]


def _glu(h, pw1_ref, pw1b_ref):
    z = _dot(h, pw1_ref[...]) + pw1b_ref[...]
    c = z.shape[-1] // 2
    return z[:, :c] * jax.nn.sigmoid(z[:, c:])


def _conv_prompt_kernel(x_ref, sh_ref, sc_ref, g1_ref, pw1_ref, pw1b_ref, dw_ref, dwb_ref, cn_ref,
                        pw2_ref, pw2b_ref, gate1_ref, g2_ref, sh2_ref, sc2_ref, rw_ref, rb_ref,
                        x1_ref, h2_ref, gates_ref, st_ref, ubuf, shifted, ybuf):
    j = pl.program_id(1)
    _, s, d = x_ref.shape
    off = CONV_TAIL - (CONV_W - 1)
    span = CONV_TAIL + s - SUBLANES

    @pl.when(j == 0)
    def _():
        ubuf[0:CONV_TAIL, :] = jnp.zeros((CONV_TAIL, d), F32)

    @pl.when(j > 0)
    def _():
        ubuf[0:CONV_TAIL, :] = ubuf[s:s + CONV_TAIL, :]

    h = _modulate(x_ref[...], g1_ref[...], sh_ref[...], sc_ref[...])
    ubuf[CONV_TAIL:CONV_TAIL + s, :] = _glu(h, pw1_ref, pw1b_ref)
    st_ref[0, 0] = ubuf[s:s + CONV_TAIL, :]
    for b in range(SUBLANES):
        n_b = CONV_TAIL + s if b == 0 else span
        shifted[b, 0:n_b, :] = ubuf[b:b + n_b, :]

    def rows_body(c, carry):
        r0 = pl.multiple_of(c * CONV_ROWS, CONV_ROWS)
        acc = jnp.zeros((CONV_ROWS, d), F32) + dwb_ref[...]
        for w in range(CONV_W):
            a, b = divmod(off + w, SUBLANES)
            acc = acc + shifted[b, pl.ds(r0 + a * SUBLANES, CONV_ROWS), :] * dw_ref[w:w + 1, :]
        ybuf[pl.ds(r0, CONV_ROWS), :] = acc
        return carry

    lax.fori_loop(0, s // CONV_ROWS, rows_body, 0)
    y = _conv_post(ybuf[...], cn_ref, pw2_ref, pw2b_ref)
    _residual_and_route(x_ref[...], y, gate1_ref[...], g2_ref[...], sh2_ref[...], sc2_ref[...],
                        rw_ref[...], rb_ref[...], x1_ref, h2_ref, gates_ref)


def _conv_prompt(x3, shift, scale, g1, cw, gate1, g2, sh2, sc2, rw, rb, s):
    n, t, d = x3.shape
    nt = t // s
    tok = lambda w: pl.BlockSpec((1, s, w), lambda i, j: (i, j, 0))
    mod = pl.BlockSpec((1, 1, d), lambda i, j: (i, 0, 0))
    full = lambda a: pl.BlockSpec(a.shape, lambda i, j: (0,) * a.ndim)
    out_shape, out_specs = _route_outputs(n, t, d, tok, s, lambda i, j: (0, i * nt + j))
    out_shape.append(jax.ShapeDtypeStruct((n, nt, CONV_TAIL, d), F32))
    out_specs.append(pl.BlockSpec((1, 1, CONV_TAIL, d), lambda i, j: (i, j, 0, 0)))
    cargs = (cw["pw1"], cw["pw1_b"], cw["dw"], cw["dw_b"], cw["norm"], cw["pw2"], cw["pw2_b"])
    x1, h2, gates, tails = pl.pallas_call(
        _conv_prompt_kernel,
        grid=(n, nt),
        in_specs=[tok(d), mod, mod, full(g1)] + [full(a) for a in cargs]
                 + [mod, full(g2), mod, mod, full(rw), full(rb)],
        out_specs=out_specs,
        out_shape=out_shape,
        scratch_shapes=[pltpu.VMEM((CONV_TAIL + s, d), F32),
                        pltpu.VMEM((SUBLANES, CONV_TAIL + s, d), F32),
                        pltpu.VMEM((s, d), F32)],
        compiler_params=_cparams("arbitrary", "arbitrary"),
        name="conv_prompt",
    )(x3, shift, scale, g1, *cargs, gate1, g2, sh2, sc2, rw, rb)
    return x1, h2, gates, tails[:, nt - 1, CONV_TAIL - (CONV_W - 1):, :]


def _conv_sample_kernel(x_ref, st_ref, sh_ref, sc_ref, g1_ref, pw1_ref, pw1b_ref, dw_ref, dwb_ref, cn_ref,
                        pw2_ref, pw2b_ref, gate1_ref, g2_ref, sh2_ref, sc2_ref, rw_ref, rb_ref,
                        x1_ref, h2_ref, gates_ref, nst_ref, full_buf):
    s, nb, d = x_ref.shape
    hist = CONV_W - 1
    x = x_ref[...]
    y = _rms(x, d) * g1_ref[...][None]
    h = (y * (1.0 + sc_ref[...]) + sh_ref[...]).reshape(s * nb, d)
    full_buf[0:hist] = st_ref[...]
    full_buf[hist:hist + s] = _glu(h, pw1_ref, pw1b_ref).reshape(s, nb, d)
    acc = jnp.zeros((s, nb, d), F32) + dwb_ref[...][None]
    for w in range(CONV_W):
        acc = acc + full_buf[w:w + s] * dw_ref[w:w + 1, :][None]
    nst_ref[...] = full_buf[s:s + hist]
    yc = _conv_post(acc.reshape(s * nb, d), cn_ref, pw2_ref, pw2b_ref).reshape(s, nb, d)
    x1 = x + gate1_ref[...] * yc
    x1_ref[...] = x1
    y2 = _rms(x1, d) * g2_ref[...][None]
    h2 = (y2 * (1.0 + sc2_ref[...]) + sh2_ref[...]).reshape(s * nb, d)
    h2_ref[...] = h2.astype(BF16).reshape(s, nb, d)
    gates_ref[...] = _router_gates_t(h2, rw_ref[...], rb_ref[...])


def _conv_sample(x_t, state_t, shift, scale, g1, cw, gate1, g2, sh2, sc2, rw, rb, nb):
    s, nseq, d = x_t.shape
    hist = CONV_W - 1
    tok = lambda w: pl.BlockSpec((s, nb, w), lambda i: (0, i, 0))
    mod = pl.BlockSpec((1, nb, d), lambda i: (0, i, 0))
    st = pl.BlockSpec((hist, nb, d), lambda i: (0, i, 0))
    full = lambda a: pl.BlockSpec(a.shape, lambda i: (0,) * a.ndim)
    cargs = (cw["pw1"], cw["pw1_b"], cw["dw"], cw["dw_b"], cw["norm"], cw["pw2"], cw["pw2_b"])
    return pl.pallas_call(
        _conv_sample_kernel,
        grid=(nseq // nb,),
        in_specs=[tok(d), st, mod, mod, full(g1)] + [full(a) for a in cargs]
                 + [mod, full(g2), mod, mod, full(rw), full(rb)],
        out_specs=[tok(d), tok(d), pl.BlockSpec((N_EXPERTS, s * nb), lambda i: (0, i)), st],
        out_shape=[jax.ShapeDtypeStruct((s, nseq, d), F32),
                   jax.ShapeDtypeStruct((s, nseq, d), BF16),
                   jax.ShapeDtypeStruct((N_EXPERTS, s * nseq), F32),
                   jax.ShapeDtypeStruct((hist, nseq, d), F32)],
        scratch_shapes=[pltpu.VMEM((hist + s, nb, d), F32)],
        compiler_params=_cparams("arbitrary"),
        name="conv_sample",
    )(x_t, state_t, shift, scale, g1, *cargs, gate1, g2, sh2, sc2, rw, rb)


def _moe_kernel(h_ref, gates_ref, x1_ref, gate2_ref, wg_ref, wu_ref, wd_ref, o_ref, acc_ref):
    e = pl.program_id(2)
    ne = pl.num_programs(2)
    nb, s, d = h_ref.shape
    tm = nb * s

    @pl.when(e == 0)
    def _():
        acc_ref[...] = jnp.zeros_like(acc_ref)

    h = h_ref[...].reshape(tm, d)
    a = _dot(h, wg_ref[0])
    u = (a * jax.nn.sigmoid(a)) * _dot(h, wu_ref[0])
    gates = gates_ref[...].reshape(tm, N_EXPERTS)
    ge = jnp.sum(jnp.where(_lane_iota(gates.shape) == e, gates, 0.0), axis=-1, keepdims=True)
    acc_ref[...] += ge * _dot(u, wd_ref[0])

    @pl.when(e == ne - 1)
    def _():
        o_ref[...] = x1_ref[...] + gate2_ref[...] * acc_ref[...].reshape(nb, s, d)


def _moe(h2, gates, x1, gate2, wg, wu, wd, nb, s):
    nb_tot, s_tot, d = h2.shape
    ne, _, de = wg.shape
    tok = lambda w: pl.BlockSpec((nb, s, w), lambda i, j, e: (i, j, 0))
    mod = pl.BlockSpec((nb, 1, d), lambda i, j, e: (i, 0, 0))
    return pl.pallas_call(
        _moe_kernel,
        grid=(nb_tot // nb, s_tot // s, ne),
        in_specs=[tok(d), tok(N_EXPERTS), tok(d), mod,
                  pl.BlockSpec((1, d, de), lambda i, j, e: (e, 0, 0)),
                  pl.BlockSpec((1, d, de), lambda i, j, e: (e, 0, 0)),
                  pl.BlockSpec((1, de, d), lambda i, j, e: (e, 0, 0))],
        out_specs=tok(d),
        out_shape=jax.ShapeDtypeStruct((nb_tot, s_tot, d), F32),
        scratch_shapes=[pltpu.VMEM((nb * s, d), F32)],
        compiler_params=_cparams("arbitrary", "arbitrary", "arbitrary"),
        name="moe",
    )(h2, gates, x1, gate2, wg, wu, wd)


MOE_TILE = 512
META_ROWS = 8


def _route_meta_kernel(g_ref, eid_ref, meta_ref, cnt_ref, run_ref):
    i = pl.program_id(0)
    tm = g_ref.shape[1]

    @pl.when(i == 0)
    def _():
        run_ref[...] = jnp.zeros_like(run_ref)

    hot = g_ref[...] > 0.0
    hotf = jnp.where(hot, 1.0, 0.0)
    before = lax.broadcasted_iota(jnp.int32, (tm, tm), 0) < lax.broadcasted_iota(jnp.int32, (tm, tm), 1)
    rank = _dot(hotf, jnp.where(before, 1.0, 0.0)) + run_ref[...]
    eid = eid_ref[...]
    e0 = jnp.min(jnp.where(hot, eid, float(N_EXPERTS - 1)), axis=0, keepdims=True)
    e1 = jnp.max(jnp.where(hot, eid, 0.0), axis=0, keepdims=True)
    p0 = jnp.sum(jnp.where(jnp.logical_and(hot, eid == e0), rank, 0.0), axis=0, keepdims=True)
    p1 = jnp.sum(jnp.where(jnp.logical_and(hot, eid == e1), rank, 0.0), axis=0, keepdims=True)
    rec = jnp.concatenate([e0, e1, p0, p1, jnp.zeros((META_ROWS - 4, tm), F32)], axis=0)
    meta_ref[...] = rec.astype(jnp.int32)
    run_ref[...] += jnp.sum(hotf, axis=1, keepdims=True)
    cnt_ref[...] = run_ref[...]


def _route_meta(gates_t, eid, tm):
    e, m = gates_t.shape
    return pl.pallas_call(
        _route_meta_kernel,
        grid=(m // tm,),
        in_specs=[pl.BlockSpec((e, tm), lambda i: (0, i)), pl.BlockSpec((e, 1), lambda i: (0, 0))],
        out_specs=[pl.BlockSpec((META_ROWS, tm), lambda i: (0, i)), pl.BlockSpec((e, 1), lambda i: (0, 0))],
        out_shape=[jax.ShapeDtypeStruct((META_ROWS, m), jnp.int32), jax.ShapeDtypeStruct((e, 1), F32)],
        scratch_shapes=[pltpu.VMEM((e, 1), F32)],
        compiler_params=_cparams("arbitrary"),
        name="route_meta",
    )(gates_t, eid)


def _row_copies(src_ref, dst_ref, rows_ref, sem, gather):
    n = rows_ref.shape[2]

    def body(r, carry):
        hbm_row = pl.ds(rows_ref[0, 0, r], 1)
        if gather:
            pltpu.make_async_copy(src_ref.at[hbm_row, :], dst_ref.at[pl.ds(r, 1), :], sem).start()
        else:
            pltpu.make_async_copy(src_ref.at[pl.ds(r, 1), :], dst_ref.at[hbm_row, :], sem).start()
        return carry

    lax.fori_loop(0, n, body, 0, unroll=8)


def _wait_rows(hbm_ref, vmem_ref, sem, gather):
    n = vmem_ref.shape[0]
    if gather:
        pltpu.make_async_copy(hbm_ref.at[pl.ds(0, n), :], vmem_ref, sem).wait()
    else:
        pltpu.make_async_copy(vmem_ref, hbm_ref.at[pl.ds(0, n), :], sem).wait()


def _dispatch_kernel(d0_ref, d1_ref, h_ref, zero_ref, xs_ref, buf, sems):
    del zero_ref
    buf[...] = h_ref[0].astype(F32)
    _row_copies(buf, xs_ref, d0_ref, sems.at[0], gather=False)
    _row_copies(buf, xs_ref, d1_ref, sems.at[1], gather=False)
    _wait_rows(xs_ref, buf, sems.at[0], gather=False)
    _wait_rows(xs_ref, buf, sems.at[1], gather=False)


def _dispatch(h2, d0, d1, rows_pad, tm):
    n, t, d = h2.shape
    nt = t // tm
    smem = pl.BlockSpec((1, 1, tm), lambda i, j: (i * nt + j, 0, 0), memory_space=pltpu.SMEM)
    hbm = pl.BlockSpec(memory_space=pl.ANY)
    return pl.pallas_call(
        _dispatch_kernel,
        grid=(n, nt),
        in_specs=[smem, smem, pl.BlockSpec((1, tm, d), lambda i, j: (i, j, 0)), hbm],
        out_specs=hbm,
        out_shape=jax.ShapeDtypeStruct((rows_pad, d), F32),
        scratch_shapes=[pltpu.VMEM((tm, d), F32), pltpu.SemaphoreType.DMA((2,))],
        input_output_aliases={3: 0},
        compiler_params=_cparams("arbitrary", "arbitrary"),
        name="moe_dispatch",
    )(d0, d1, h2, jnp.zeros((rows_pad, d), F32))


def _expert_tile_kernel(te_ref, tv_ref, x_ref, wg_ref, wu_ref, wd_ref, y_ref):
    del te_ref
    t = pl.program_id(0)

    @pl.when(tv_ref[t] != 0)
    def _():
        x = x_ref[...]
        a = _dot(x, wg_ref[0])
        u = (a * jax.nn.sigmoid(a)) * _dot(x, wu_ref[0])
        y_ref[...] = _dot(u, wd_ref[0])

    @pl.when(tv_ref[t] == 0)
    def _():
        y_ref[...] = jnp.zeros_like(y_ref)


def _expert_tiles(xs, tile_expert, tile_valid, wg, wu, wd):
    rows, d = xs.shape
    _, _, de = wg.shape
    grid_spec = pltpu.PrefetchScalarGridSpec(
        num_scalar_prefetch=2,
        grid=(rows // MOE_TILE,),
        in_specs=[pl.BlockSpec((MOE_TILE, d), lambda t, te, tv: (t, 0)),
                  pl.BlockSpec((1, d, de), lambda t, te, tv: (te[t], 0, 0)),
                  pl.BlockSpec((1, d, de), lambda t, te, tv: (te[t], 0, 0)),
                  pl.BlockSpec((1, de, d), lambda t, te, tv: (te[t], 0, 0))],
        out_specs=pl.BlockSpec((MOE_TILE, d), lambda t, te, tv: (t, 0)),
    )
    return pl.pallas_call(
        _expert_tile_kernel,
        grid_spec=grid_spec,
        out_shape=jax.ShapeDtypeStruct((rows, d), F32),
        compiler_params=_cparams("arbitrary"),
        name="moe_expert_tiles",
    )(tile_expert, tile_valid, xs, wg, wu, wd)


def _combine_kernel(d0_ref, d1_ref, y_ref, x1_ref, gate2_ref, gates_ref, o_ref, b0, b1, sems):
    _row_copies(y_ref, b0, d0_ref, sems.at[0], gather=True)
    _row_copies(y_ref, b1, d1_ref, sems.at[1], gather=True)
    g = gates_ref[0]
    lane = _lane_iota(g.shape).astype(F32)
    first = jnp.min(jnp.where(g > 0.0, lane, float(N_EXPERTS)), axis=-1, keepdims=True)
    w0 = jnp.sum(jnp.where(lane == first, g, 0.0), axis=-1, keepdims=True)
    w1 = jnp.sum(g, axis=-1, keepdims=True) - w0
    _wait_rows(y_ref, b0, sems.at[0], gather=True)
    _wait_rows(y_ref, b1, sems.at[1], gather=True)
    o_ref[0] = x1_ref[0] + gate2_ref[0] * (w0 * b0[...] + w1 * b1[...])


def _combine(y, d0, d1, x1, gate2, gates, tm):
    n, t, d = x1.shape
    nt = t // tm
    smem = pl.BlockSpec((1, 1, tm), lambda i, j: (i * nt + j, 0, 0), memory_space=pltpu.SMEM)
    tok = lambda w: pl.BlockSpec((1, tm, w), lambda i, j: (i, j, 0))
    return pl.pallas_call(
        _combine_kernel,
        grid=(n, nt),
        in_specs=[smem, smem, pl.BlockSpec(memory_space=pl.ANY), tok(d),
                  pl.BlockSpec((1, 1, d), lambda i, j: (i, 0, 0)), tok(N_EXPERTS)],
        out_specs=tok(d),
        out_shape=jax.ShapeDtypeStruct((n, t, d), F32),
        scratch_shapes=[pltpu.VMEM((tm, d), F32), pltpu.VMEM((tm, d), F32), pltpu.SemaphoreType.DMA((2,))],
        compiler_params=_cparams("arbitrary", "arbitrary"),
        name="moe_combine",
    )(d0, d1, y, x1, gate2, gates)


def _moe_sorted(h2, gates_t, x1, gate2, wg, wu, wd, tm):
    n, t, d = h2.shape
    m = n * t
    rows = jnp.arange(N_EXPERTS)
    eid = (rows % N_GROUPS) * E_PER_GROUP + rows // N_GROUPS
    meta, cnt_rows = _route_meta(gates_t, eid.astype(F32)[:, None], tm)
    row_of_expert = (rows % E_PER_GROUP) * N_GROUPS + rows // E_PER_GROUP
    cnt = cnt_rows[:, 0].astype(jnp.int32)[row_of_expert]
    tiles = (cnt + MOE_TILE - 1) // MOE_TILE
    tile_end = jnp.cumsum(tiles)
    start_row = (tile_end - tiles) * MOE_TILE
    n_tiles = (2 * m) // MOE_TILE + N_EXPERTS
    tile_id = jnp.arange(n_tiles)
    tile_valid = (tile_id < tile_end[-1]).astype(jnp.int32)
    tile_expert = jnp.minimum(jnp.sum(tile_id[:, None] >= tile_end[None, :], axis=1), N_EXPERTS - 1).astype(jnp.int32)
    d0 = (start_row[meta[0]] + meta[2]).reshape(m // tm, 1, tm)
    d1 = (start_row[meta[1]] + meta[3]).reshape(m // tm, 1, tm)
    xs = _dispatch(h2, d0, d1, n_tiles * MOE_TILE, tm)
    y = _expert_tiles(xs, tile_expert, tile_valid, wg, wu, wd)
    return _combine(y, d0, d1, x1, gate2, _gates_dense(gates_t).reshape(n, t, N_EXPERTS), tm)


def _tied(g):
    return jnp.concatenate([g, g[QK_NOPE:], jnp.zeros((HEAD_PAD - QK_DIM,), F32)])[None]


def _rope_tables(pos):
    half = QK_ROPE // 2
    inv = ROPE_BASE ** (-jnp.arange(0, QK_ROPE, 2, dtype=F32) / QK_ROPE)
    ang = pos.astype(F32)[:, None] * inv[None, :]
    cos, sin = jnp.cos(ang), jnp.sin(ang)
    t = pos.shape[0]
    z = lambda n: jnp.zeros((t, n), F32)
    one = jnp.ones((t, QK_NOPE), F32)
    q_cos = jnp.concatenate([one, cos, cos, z(HEAD_PAD - QK_DIM)], axis=1)
    q_sa = jnp.concatenate([z(QK_NOPE), -sin, z(half), z(HEAD_PAD - QK_DIM)], axis=1)
    q_sb = jnp.concatenate([z(QK_NOPE), z(half), sin, z(HEAD_PAD - QK_DIM)], axis=1)
    k_cos = jnp.concatenate([cos, cos, z(LANES - QK_ROPE)], axis=1)
    k_sa = jnp.concatenate([-sin, z(LANES - half)], axis=1)
    k_sb = jnp.concatenate([z(half), sin, z(LANES - QK_ROPE)], axis=1)
    return jnp.stack([q_cos, q_sa, q_sb, k_cos, k_sa, k_sb])


def _attn_weights(w_in, q_a_norm, w_q_b, kv_a_norm, w_kv_b, mla_q_gain, mla_k_gain, diff_q_gain, diff_k_gain):
    d = w_in.shape[0]
    c1 = Q_LORA + KV_LORA
    w_in_p = jnp.concatenate([w_in[:, :c1], w_in[:, c1:c1 + QK_ROPE], jnp.zeros((d, LANES - QK_ROPE), F32),
                              w_in[:, c1 + QK_ROPE:]], axis=1).astype(BF16)
    wqb = jnp.pad(w_q_b.reshape(Q_LORA, MLA_HEADS, QK_DIM), ((0, 0), (0, 0), (0, HEAD_PAD - QK_DIM)))
    wkv = w_kv_b.reshape(KV_LORA, MLA_HEADS, QK_NOPE + V_HEAD)
    wkn = jnp.pad(wkv[:, :, :QK_NOPE], ((0, 0), (0, 0), (0, HEAD_PAD - QK_NOPE)))
    wv = wkv[:, :, QK_NOPE:].reshape(KV_LORA, MLA_OUT)
    return {
        "w_in": w_in_p,
        "q_a_norm": q_a_norm[None],
        "w_q_b": wqb.reshape(Q_LORA, MLA_HEADS * HEAD_PAD).astype(BF16),
        "kv_a_norm": kv_a_norm[None],
        "w_kv": jnp.concatenate([wkn.reshape(KV_LORA, MLA_HEADS * HEAD_PAD), wv], axis=1).astype(BF16),
        "w_kb": wkv[:, :, :QK_NOPE].reshape(KV_LORA, MLA_HEADS * QK_NOPE).astype(BF16),
        "w_v": wv.astype(BF16),
        "q_gain": _tied(mla_q_gain),
        "k_gain": _tied(mla_k_gain),
        "dq_gain": jnp.concatenate([diff_q_gain, diff_q_gain])[None],
        "dk_gain": jnp.concatenate([diff_k_gain, diff_k_gain])[None],
    }


PROMPT_TILE = 512
SAMPLE_SEQS = 64


def kernel(x_prompt, x_sample, cache_mla_latent, cache_mla_kpe, cache_diff_k, cache_diff_v, state_conv, page_table, c_prompt, c_sample, norm1, norm2, ada_w, ada_b, w_in, q_a_norm, w_q_b, kv_a_norm, w_kv_b, mla_q_gain, mla_k_gain, diff_q_gain, diff_k_gain, diff_lambda, diff_subln, w_o, conv_pw1, conv_pw1_b, conv_dw, conv_dw_b, conv_norm, conv_pw2, conv_pw2_b, router_w, router_bias, moe_w_gate, moe_w_up, moe_w_down):
    n, t, d = x_prompt.shape
    nseq, n_new, _ = x_sample.shape
    n_pages = page_table.shape[1]
    depth = norm1.shape[0]
    sp = min(PROMPT_TILE, t)
    sb = min(SAMPLE_SEQS, nseq)

    rows = -(-(n + nseq) // 8) * 8
    c_all = jnp.concatenate([c_prompt, c_sample, jnp.zeros((rows - n - nseq, d), F32)], axis=0)
    ada = _adaln(c_all, ada_w, ada_b).reshape(depth, rows, 6, d)

    router_w, rb = _router_layout(router_w, router_bias)
    xp, xs = x_prompt, x_sample
    outs = {k: [] for k in ("lat_p", "lat_s", "kpe_p", "kpe_s", "dk_p", "dk_s", "dv_p", "dv_s", "cv_p", "cv_s")}
    for l in range(depth):
        mp = [ada[l, :n, i][:, None, :] for i in range(6)]
        ms = [ada[l, n:n + nseq, i][:, None, :] for i in range(6)]
        g1, g2 = norm1[l][None], norm2[l][None]
        wg, wu, wd = moe_w_gate[l], moe_w_up[l], moe_w_down[l]
        if l % 2 == 0:
            a = l // 2
            lam_init = 0.8 - 0.6 * math.exp(-0.3 * l)
            wts = _attn_weights(w_in[a], q_a_norm[a], w_q_b[a], kv_a_norm[a], w_kv_b[a], mla_q_gain[a],
                                mla_k_gain[a], diff_q_gain[a], diff_k_gain[a])
            subln = diff_subln[a][None]
            w_o_b = w_o[a].astype(BF16)
            tab_p = _rope_tables(jnp.arange(t))
            q, ckv, kpe, dq, dk, dv, k, v, dkb, dvb = _attn_project(True, xp, mp[0], mp[1], g1, wts, tab_p, 1, sp)
            om = _mla_prompt_attention(q, k, v)
            od = _diff_prompt_attention(dq, dkb, dvb, diff_lambda[a], subln, lam_init)
            xp1, hp2, gp = _attn_merge(xp, om, od, w_o_b, mp[2], g2, mp[3], mp[4], router_w, rb, 1, sp)
            outs["lat_p"].append(ckv); outs["kpe_p"].append(kpe)
            outs["dk_p"].append(dk.reshape(n, t, DIFF_HEADS, 2 * DIFF_HD))
            outs["dv_p"].append(dv.reshape(n, t, DIFF_HEADS, 2 * DIFF_HD))
            tab_s = _rope_tables(n_pages * PAGE_SIZE + jnp.arange(n_new))
            q, ckv, kpe, dq, dk, dv = _attn_project(False, xs, ms[0], ms[1], g1, wts, tab_s, sb, n_new)
            qrow = q.reshape(nseq, n_new, MLA_HEADS, HEAD_PAD).transpose(0, 2, 1, 3).reshape(
                nseq, MLA_HEADS * n_new, HEAD_PAD)
            dqrow = dq.reshape(nseq, n_new, 2 * DIFF_HEADS, DIFF_HD).transpose(0, 2, 1, 3).reshape(
                nseq, 2 * DIFF_HEADS * n_new, DIFF_HD)
            hp_shape = cache_diff_k.shape[:2] + (PAGE_SIZE * DIFF_HEADS, 2 * DIFF_HD)
            om, od = _sample_attention(
                a, page_table, qrow, dqrow, ckv, kpe.transpose(0, 2, 1), dk, dv,
                cache_mla_latent, jnp.swapaxes(cache_mla_kpe, 2, 3),
                cache_diff_k.reshape(hp_shape), cache_diff_v.reshape(hp_shape),
                wts["w_kb"], wts["w_v"], wts["k_gain"], diff_lambda[a], subln, lam_init)
            xs1, hs2, gs = _attn_merge(xs, om, od, w_o_b, ms[2], g2, ms[3], ms[4], router_w, rb, sb, n_new)
            outs["lat_s"].append(ckv); outs["kpe_s"].append(kpe)
            outs["dk_s"].append(dk.reshape(nseq, n_new, DIFF_HEADS, 2 * DIFF_HD))
            outs["dv_s"].append(dv.reshape(nseq, n_new, DIFF_HEADS, 2 * DIFF_HD))
        else:
            c = l // 2
            cw = {"pw1": conv_pw1[c].astype(BF16), "pw1_b": conv_pw1_b[c][None], "dw": conv_dw[c],
                  "dw_b": conv_dw_b[c][None], "norm": conv_norm[c][None], "pw2": conv_pw2[c].astype(BF16),
                  "pw2_b": conv_pw2_b[c][None]}
            xp1, hp2, gp, st_p = _conv_prompt(xp, mp[0], mp[1], g1, cw, mp[2], g2, mp[3], mp[4], router_w, rb,
                                              min(256, t))
            outs["cv_p"].append(st_p)
            tm_ = lambda m: m.transpose(1, 0, 2)
            xs1_t, hs2_t, gs_t, st_s = _conv_sample(
                xs.transpose(1, 0, 2), state_conv[c].transpose(1, 0, 2), tm_(ms[0]), tm_(ms[1]), g1, cw,
                tm_(ms[2]), g2, tm_(ms[3]), tm_(ms[4]), router_w, rb, min(32, nseq))
            xs1, hs2 = xs1_t.transpose(1, 0, 2), hs2_t.transpose(1, 0, 2)
            nbc = min(32, nseq)
            gs = gs_t.reshape(N_EXPERTS, nseq // nbc, n_new, nbc).transpose(0, 1, 3, 2).reshape(N_EXPERTS, -1)
            outs["cv_s"].append(st_s.transpose(1, 0, 2))
        xp = _moe_sorted(hp2, gp, xp1, mp[5], wg, wu, wd, min(MOE_TILE, t))
        xs = _moe(hs2, _gates_dense(gs).reshape(nseq, n_new, N_EXPERTS), xs1, ms[5], wg, wu, wd, nseq, n_new)
    st = lambda k: jnp.stack(outs[k])
    return (xp, xs, st("lat_p"), st("lat_s"), st("kpe_p"), st("kpe_s"), st("dk_p"), st("dk_s"),
            st("dv_p"), st("dv_s"), st("cv_p"), st("cv_s"))
```

```python
import functools
import math

import jax
import jax.numpy as jnp
from jax import lax
from jax.experimental import pallas as pl
from jax.experimental.pallas import tpu as pltpu

F32 = jnp.float32
BF16 = jnp.bfloat16

D_MODEL = 1024
PAGE_SIZE = 128
MLA_HEADS = 8
QK_NOPE = 64
QK_ROPE = 32
QK_DIM = QK_NOPE + QK_ROPE
V_HEAD = 64
Q_LORA = 384
KV_LORA = 256
ROPE_BASE = 10000.0
MLA_OUT = MLA_HEADS * V_HEAD
DIFF_HEADS = 4
DIFF_HD = 64
DIFF_W = DIFF_HEADS * 2 * DIFF_HD
CONV_W = 31
N_EXPERTS = 16
N_GROUPS = 4
E_PER_GROUP = N_EXPERTS // N_GROUPS
D_EXPERT = 512
EPS = 1e-6
LOG2E = math.log2(math.e)

LANES = 128
HEAD_PAD = 128
VMEM_LIMIT = 48 * 1024 * 1024


def _cparams(*sem):
    return pltpu.CompilerParams(dimension_semantics=sem, vmem_limit_bytes=VMEM_LIMIT)


def _dot(a, b):
    return jnp.dot(a.astype(BF16), b.astype(BF16), preferred_element_type=F32)


def _dot_nt(a, b):
    return lax.dot_general(a.astype(BF16), b.astype(BF16), (((1,), (1,)), ((), ())),
                           preferred_element_type=F32)


def _dot_tn(a, b):
    return lax.dot_general(a.astype(BF16), b.astype(BF16), (((0,), (0,)), ((), ())),
                           preferred_element_type=F32)


def _rms(x, n):
    return x * lax.rsqrt(jnp.sum(x * x, axis=-1, keepdims=True) * (1.0 / n) + EPS)


def _lane_iota(shape):
    return lax.broadcasted_iota(jnp.int32, shape, len(shape) - 1)


def _adaln_kernel(c_ref, w_ref, b_ref, o_ref):
    c = c_ref[...]
    s = c * jax.nn.sigmoid(c)
    o_ref[0] = _dot(s, w_ref[0]) + b_ref[0]


def _adaln(c_all, ada_w, ada_b):
    n_layers, d, d6 = ada_w.shape
    rows = c_all.shape[0]
    tn = 1536
    return pl.pallas_call(
        _adaln_kernel,
        grid=(n_layers, d6 // tn),
        in_specs=[pl.BlockSpec((rows, d), lambda l, j: (0, 0)),
                  pl.BlockSpec((1, d, tn), lambda l, j: (l, 0, j)),
                  pl.BlockSpec((1, 1, tn), lambda l, j: (l, 0, j))],
        out_specs=pl.BlockSpec((1, rows, tn), lambda l, j: (l, 0, j)),
        out_shape=jax.ShapeDtypeStruct((n_layers, rows, d6), F32),
        compiler_params=_cparams("arbitrary", "arbitrary"),
        name="adaln",
    )(c_all, ada_w, ada_b.reshape(n_layers, 1, d6))


def _modulate(x3, g, shift, scale):
    nb, s, d = x3.shape
    y = _rms(x3, d) * g[None]
    y = y * (1.0 + scale) + shift
    return y.reshape(nb * s, d)


def _bcast_rows(tab, nb):
    s, l = tab.shape
    return jnp.broadcast_to(tab[None], (nb, s, l)).reshape(nb * s, l)


def _rope_block(x, cos_t, sin_a, sin_b, half):
    n = x.shape[-1]
    return x * cos_t + pltpu.roll(x, n - half, 1) * sin_a + pltpu.roll(x, half, 1) * sin_b


def _router_gates_t(h2, rwt, rbt):
    tm = h2.shape[0]
    h_hi = h2.astype(BF16)
    h_lo = (h2 - h_hi.astype(F32)).astype(BF16)
    by_hi = _dot_nt(rwt, h_hi)
    logits = by_hi[:N_EXPERTS] + by_hi[N_EXPERTS:] + _dot_nt(rwt[:N_EXPERTS], h_lo)
    scores = jax.nn.sigmoid(logits)
    sel = scores + rbt
    a = [sel[i * N_GROUPS:(i + 1) * N_GROUPS] for i in range(E_PER_GROUP)]
    sc = [scores[i * N_GROUPS:(i + 1) * N_GROUPS] for i in range(E_PER_GROUP)]
    member = []
    for i in range(E_PER_GROUP):
        rank = jnp.zeros((N_GROUPS, tm), F32)
        for j in range(E_PER_GROUP):
            if j != i:
                ahead = (a[j] >= a[i]) if j < i else (a[j] > a[i])
                rank = rank + jnp.where(ahead, 1.0, 0.0)
        member.append(rank < 2.0)
    gscore = jnp.zeros((N_GROUPS, tm), F32)
    for i in range(E_PER_GROUP):
        gscore = gscore + jnp.where(member[i], a[i], 0.0)
    best = gscore[0:1]
    best_g = jnp.zeros((1, tm), jnp.int32)
    for g in range(1, N_GROUPS):
        better = gscore[g:g + 1] > best
        best = jnp.where(better, gscore[g:g + 1], best)
        best_g = jnp.where(better, g, best_g)
    in_group = lax.broadcasted_iota(jnp.int32, (N_GROUPS, tm), 0) == best_g
    picked = [jnp.where(in_group, jnp.where(member[i], sc[i], 0.0), 0.0) for i in range(E_PER_GROUP)]
    total = picked[0]
    for i in range(1, E_PER_GROUP):
        total = total + picked[i]
    inv = 1.0 / jnp.sum(total, axis=0, keepdims=True)
    return jnp.concatenate([p * inv for p in picked], axis=0)


def _residual_and_route(x3, y, gate1, g2, sh2, sc2, rwt, rbt, x1_ref, h2_ref, gates_ref):
    nb, s, d = x3.shape
    x1 = x3 + gate1 * y.reshape(nb, s, d)
    x1_ref[...] = x1
    h2 = _modulate(x1, g2, sh2, sc2)
    h2_ref[...] = h2.astype(BF16).reshape(nb, s, d)
    gates_ref[...] = _router_gates_t(h2, rwt, rbt)


def _gates_dense(gates_t):
    m = gates_t.shape[1]
    return gates_t.reshape(E_PER_GROUP, N_GROUPS, m).transpose(2, 1, 0).reshape(m, N_EXPERTS)


def _router_layout(router_w, router_bias):
    d = router_w.shape[0]
    rwt = router_w.T.reshape(N_GROUPS, E_PER_GROUP, d).transpose(1, 0, 2).reshape(N_EXPERTS, d)
    rbt = router_bias.reshape(N_GROUPS, E_PER_GROUP).T.reshape(N_EXPERTS, 1)
    hi = rwt.astype(BF16)
    lo = (rwt - hi.astype(F32)).astype(BF16)
    return jnp.concatenate([hi, lo], axis=0), rbt


IN_PAD = Q_LORA + KV_LORA + LANES + 3 * DIFF_W
C_QA, C_CKV, C_KPE, C_DQ, C_DK, C_DV = 0, Q_LORA, Q_LORA + KV_LORA, 768, 1280, 1792


def _pair_norm(x, gain2):
    outs = []
    for h in range(DIFF_HEADS):
        blk = x[:, h * LANES:(h + 1) * LANES]
        lo = _lane_iota(blk.shape) < DIFF_HD
        sq = blk * blk
        s_lo = jnp.sum(jnp.where(lo, sq, 0.0), axis=-1, keepdims=True)
        s_all = jnp.sum(sq, axis=-1, keepdims=True)
        r_lo = lax.rsqrt(s_lo * (1.0 / DIFF_HD) + EPS)
        r_hi = lax.rsqrt((s_all - s_lo) * (1.0 / DIFF_HD) + EPS)
        outs.append(blk * jnp.where(lo, r_lo, r_hi) * gain2)
    return jnp.concatenate(outs, axis=1)


def _proj_kernel(prompt, x_ref, sh_ref, sc_ref, g1_ref, win_ref, qan_ref, wqb_ref, kvn_ref, wkv_ref,
                 qg_ref, kg_ref, dqg_ref, dkg_ref, tab_ref, *outs):
    if prompt:
        q_ref, ckv_ref, kpe_ref, dq_ref, dk_ref, dv_ref, k_ref, v_ref, dkb_ref, dvb_ref = outs
    else:
        q_ref, ckv_ref, kpe_ref, dq_ref, dk_ref, dv_ref = outs
    nb, s, _ = x_ref.shape
    tm = nb * s
    h = _modulate(x_ref[...], g1_ref[...], sh_ref[...], sc_ref[...])
    z = _dot(h, win_ref[...])
    tab = tab_ref[...]
    qcos, qsa, qsb, kcos, ksa, ksb = [_bcast_rows(tab[i], nb) for i in range(6)]

    ckv = _rms(z[:, C_CKV:C_CKV + KV_LORA], KV_LORA) * kvn_ref[...]
    ckv_ref[...] = ckv.reshape(nb, s, KV_LORA)
    kpe = _rope_block(z[:, C_KPE:C_KPE + LANES], kcos, ksa, ksb, QK_ROPE // 2)
    kpe_ref[...] = kpe[:, :QK_ROPE].reshape(nb, s, QK_ROPE)

    qa = _rms(z[:, C_QA:C_QA + Q_LORA], Q_LORA) * qan_ref[...]
    qh = _dot(qa, wqb_ref[...])
    q_scale = QK_DIM ** -0.5 * LOG2E if prompt else 1.0
    q_out = []
    for hd in range(MLA_HEADS):
        blk = qh[:, hd * HEAD_PAD:(hd + 1) * HEAD_PAD]
        blk = _rope_block(blk, qcos, qsa, qsb, QK_ROPE // 2)
        q_out.append(_rms(blk, QK_DIM) * (qg_ref[...] * q_scale))
    q_ref[...] = jnp.concatenate(q_out, axis=1).astype(q_ref.dtype).reshape(nb, s, MLA_HEADS * HEAD_PAD)

    d_scale = DIFF_HD ** -0.5 * LOG2E if prompt else 1.0
    dq = _pair_norm(z[:, C_DQ:C_DQ + DIFF_W], dqg_ref[...] * d_scale)
    dq_ref[...] = dq.astype(dq_ref.dtype).reshape(nb, s, DIFF_W)
    dk = _pair_norm(z[:, C_DK:C_DK + DIFF_W], dkg_ref[...])
    dk_ref[...] = dk.reshape(nb, s, DIFF_W)
    dv = z[:, C_DV:C_DV + DIFF_W]
    dv_ref[...] = dv.reshape(nb, s, DIFF_W)

    if prompt:
        dkb_ref[...] = dk.astype(BF16).reshape(nb, s, DIFF_W)
        dvb_ref[...] = dv.astype(BF16).reshape(nb, s, DIFF_W)
        kv = _dot(ckv, wkv_ref[...])
        kpe_sh = pltpu.roll(kpe, QK_NOPE, 1)
        k_out = []
        for hd in range(MLA_HEADS):
            blk = kv[:, hd * HEAD_PAD:(hd + 1) * HEAD_PAD] + kpe_sh
            k_out.append(_rms(blk, QK_DIM) * kg_ref[...])
        k_ref[...] = jnp.concatenate(k_out, axis=1).astype(BF16).reshape(nb, s, MLA_HEADS * HEAD_PAD)
        v_ref[...] = kv[:, MLA_HEADS * HEAD_PAD:].astype(BF16).reshape(nb, s, MLA_OUT)


def _attn_project(prompt, x3, shift, scale, g1, wts, tab, nb, s):
    nb_tot, s_tot, d = x3.shape
    grid = (nb_tot // nb, s_tot // s)
    tok = lambda w: pl.BlockSpec((nb, s, w), lambda i, j: (i, j, 0))
    mod = pl.BlockSpec((nb, 1, d), lambda i, j: (i, 0, 0))
    full = lambda a: pl.BlockSpec(a.shape, lambda i, j: (0,) * a.ndim)
    qdt = BF16 if prompt else F32
    out_shape = [jax.ShapeDtypeStruct((nb_tot, s_tot, MLA_HEADS * HEAD_PAD), qdt),
                 jax.ShapeDtypeStruct((nb_tot, s_tot, KV_LORA), F32),
                 jax.ShapeDtypeStruct((nb_tot, s_tot, QK_ROPE), F32),
                 jax.ShapeDtypeStruct((nb_tot, s_tot, DIFF_W), qdt),
                 jax.ShapeDtypeStruct((nb_tot, s_tot, DIFF_W), F32),
                 jax.ShapeDtypeStruct((nb_tot, s_tot, DIFF_W), F32)]
    out_specs = [tok(MLA_HEADS * HEAD_PAD), tok(KV_LORA), tok(QK_ROPE), tok(DIFF_W), tok(DIFF_W), tok(DIFF_W)]
    if prompt:
        out_shape += [jax.ShapeDtypeStruct((nb_tot, s_tot, MLA_HEADS * HEAD_PAD), BF16),
                      jax.ShapeDtypeStruct((nb_tot, s_tot, MLA_OUT), BF16),
                      jax.ShapeDtypeStruct((nb_tot, s_tot, DIFF_W), BF16),
                      jax.ShapeDtypeStruct((nb_tot, s_tot, DIFF_W), BF16)]
        out_specs += [tok(MLA_HEADS * HEAD_PAD), tok(MLA_OUT), tok(DIFF_W), tok(DIFF_W)]
    args = (g1, wts["w_in"], wts["q_a_norm"], wts["w_q_b"], wts["kv_a_norm"], wts["w_kv"],
            wts["q_gain"], wts["k_gain"], wts["dq_gain"], wts["dk_gain"])
    return pl.pallas_call(
        functools.partial(_proj_kernel, prompt),
        grid=grid,
        in_specs=[tok(d), mod, mod] + [full(a) for a in args]
                 + [pl.BlockSpec((6, s, LANES), lambda i, j: (0, j, 0))],
        out_specs=out_specs,
        out_shape=out_shape,
        compiler_params=_cparams("arbitrary", "arbitrary"),
        name="attn_project_prompt" if prompt else "attn_project_sample",
    )(x3, shift, scale, *args, tab)


ATT_BLK = 1024
ATT_SUB = 512


def _flash_step(q, k, v, m, acc, mask=None):
    s = _dot_nt(q, k)
    if mask is not None:
        s = jnp.where(mask, s, -jnp.inf)
    m_new = jnp.maximum(m, jnp.max(s, axis=-1, keepdims=True))
    alpha = jnp.exp2(m - m_new)
    p = jnp.exp2(s - m_new)
    return m_new, alpha * acc + _dot(p, v)


def _causal_mask(n):
    return lax.broadcasted_iota(jnp.int32, (n, n), 1) <= lax.broadcasted_iota(jnp.int32, (n, n), 0)


def _flash_init(rows, width):
    return jnp.full((rows, 1), -jnp.inf, F32), jnp.zeros((rows, width), F32)


def _flash_diagonal(q, kv_rows, carry, blk, sub):
    m, acc = carry
    ms, accs = [], []
    tri = _causal_mask(sub)
    for r in range(blk // sub):
        rows = slice(r * sub, (r + 1) * sub)
        c_r = (m[rows], acc[rows])
        for c in range(r + 1):
            c_r = _flash_step(q[rows], *kv_rows(c), *c_r, mask=tri if c == r else None)
        ms.append(c_r[0])
        accs.append(c_r[1])
    return jnp.concatenate(ms, axis=0), jnp.concatenate(accs, axis=0)


def _mla_prompt_kernel(blk, sub, q_ref, k_ref, v_ref, o_ref):
    i = pl.program_id(2)
    heads = (0, 1)
    qs = [q_ref[0, :, hh * HEAD_PAD:(hh + 1) * HEAD_PAD] for hh in heads]

    def kv(start, size, hh):
        rows = pl.ds(pl.multiple_of(start, size), size)
        k = k_ref[0, rows, hh * HEAD_PAD:(hh + 1) * HEAD_PAD]
        vb = v_ref[0, rows, :]
        lane = _lane_iota(vb.shape)
        own = (lane < V_HEAD) if hh == 0 else (lane >= V_HEAD)
        one_at = V_HEAD if hh == 0 else 0
        v = jnp.where(own, vb, jnp.where(lane == one_at, 1.0, 0.0).astype(vb.dtype))
        return k, v

    def body(j, carry):
        return tuple(_flash_step(qs[hh], *kv(j * blk, blk, hh), *carry[hh]) for hh in heads)

    init = _flash_init(blk, 2 * V_HEAD)
    carry = lax.fori_loop(0, i, body, (init, init))
    accs = [_flash_diagonal(qs[hh], lambda c, hh=hh: kv(i * blk + c * sub, sub, hh), carry[hh], blk, sub)[1]
            for hh in heads]
    lane = _lane_iota(accs[0].shape)
    out = jnp.where(lane < V_HEAD, accs[0] / accs[0][:, V_HEAD:V_HEAD + 1], accs[1] / accs[1][:, 0:1])
    o_ref[0] = out.astype(o_ref.dtype)


def _mla_prompt_attention(q, k, v):
    n, t, _ = q.shape
    blk = min(ATT_BLK, t)
    sub = min(ATT_SUB, blk)
    return pl.pallas_call(
        functools.partial(_mla_prompt_kernel, blk, sub),
        grid=(n, MLA_HEADS // 2, t // blk),
        in_specs=[pl.BlockSpec((1, blk, 2 * HEAD_PAD), lambda b, h, i: (b, i, h)),
                  pl.BlockSpec((1, t, 2 * HEAD_PAD), lambda b, h, i: (b, 0, h)),
                  pl.BlockSpec((1, t, 2 * V_HEAD), lambda b, h, i: (b, 0, h))],
        out_specs=pl.BlockSpec((1, blk, 2 * V_HEAD), lambda b, h, i: (b, i, h)),
        out_shape=jax.ShapeDtypeStruct((n, t, MLA_OUT), BF16),
        compiler_params=_cparams("arbitrary", "arbitrary", "arbitrary"),
        name="mla_prompt_attention",
    )(q, k, v)


def _diff_lambda(lp_ref, lam_init):
    lp = lp_ref[...]
    a = jnp.sum(lp[0:1] * lp[1:2], axis=-1, keepdims=True)
    b = jnp.sum(lp[2:3] * lp[3:4], axis=-1, keepdims=True)
    return jnp.exp(a) - jnp.exp(b) + lam_init


ALIBI_SPLIT = 3


def _alibi_tables(t):
    kpos = jnp.arange(t)
    a, b = (kpos // 64).astype(F32), (kpos % 64).astype(F32)
    pad = jnp.zeros((t, LANES - 2 * ALIBI_SPLIT), F32)
    k_tab = jnp.concatenate([a[:, None]] * ALIBI_SPLIT + [b[:, None]] * ALIBI_SPLIT + [pad], axis=1).astype(BF16)
    pieces, rest = [], jnp.asarray(LOG2E, F32)
    for _ in range(ALIBI_SPLIT):
        p = rest.astype(BF16).astype(F32)
        pieces.append(p)
        rest = rest - p
    c = jnp.stack(pieces)
    slopes = 2.0 ** (-8.0 * jnp.arange(1, DIFF_HEADS + 1, dtype=F32) / DIFF_HEADS)
    q_tab = jnp.concatenate([64.0 * slopes[:, None] * c[None], slopes[:, None] * c[None],
                             jnp.zeros((DIFF_HEADS, LANES - 2 * ALIBI_SPLIT), F32)], axis=1)
    return k_tab, q_tab[:, None, :].astype(BF16)


def _diff_prompt_kernel(lam_init, blk, sub, q_ref, k_ref, v_ref, kt_ref, qt_ref, lp_ref, sub_ref, o_ref):
    i = pl.program_id(2)
    width = 2 * DIFF_HD
    q_all = q_ref[0]
    lo = _lane_iota(q_all.shape) < DIFF_HD
    q_pos = jnp.broadcast_to(qt_ref[0], q_all.shape)
    qs = (jnp.concatenate([jnp.where(lo, q_all, 0), q_pos], axis=1),
          jnp.concatenate([jnp.where(lo, 0, q_all), q_pos], axis=1))

    def kv(start, size):
        rows = pl.ds(pl.multiple_of(start, size), size)
        ones = jnp.where(_lane_iota((size, LANES)) == 0, 1.0, 0.0).astype(BF16)
        k = jnp.concatenate([k_ref[0, rows, :], kt_ref[rows, :]], axis=1)
        v = jnp.concatenate([v_ref[0, rows, :], ones], axis=1)
        return k, v

    def body(j, carry):
        k, v = kv(j * blk, blk)
        return tuple(_flash_step(qs[c], k, v, *carry[c]) for c in range(2))

    init = _flash_init(blk, width + LANES)
    carry = lax.fori_loop(0, i, body, (init, init))
    a0, a1 = [_flash_diagonal(qs[c], lambda r: kv(i * blk + r * sub, sub), carry[c], blk, sub)[1] for c in range(2)]
    lam = _diff_lambda(lp_ref, lam_init)
    od = a0[:, :width] / a0[:, width:width + 1] - lam * (a1[:, :width] / a1[:, width:width + 1])
    od = _rms(od, width) * sub_ref[...] * (1.0 - lam_init)
    o_ref[0] = od.astype(o_ref.dtype)


def _diff_prompt_attention(dq, dk, dv, diff_lambda, subln, lam_init):
    n, t, _ = dq.shape
    blkw = 2 * DIFF_HD
    blk = min(ATT_BLK, t)
    sub = min(ATT_SUB, blk)
    k_tab, q_tab = _alibi_tables(t)
    return pl.pallas_call(
        functools.partial(_diff_prompt_kernel, lam_init, blk, sub),
        grid=(n, DIFF_HEADS, t // blk),
        in_specs=[pl.BlockSpec((1, blk, blkw), lambda b, h, i: (b, i, h)),
                  pl.BlockSpec((1, t, blkw), lambda b, h, i: (b, 0, h)),
                  pl.BlockSpec((1, t, blkw), lambda b, h, i: (b, 0, h)),
                  pl.BlockSpec((t, LANES), lambda b, h, i: (0, 0)),
                  pl.BlockSpec((1, 1, LANES), lambda b, h, i: (h, 0, 0)),
                  pl.BlockSpec(diff_lambda.shape, lambda b, h, i: (0, 0)),
                  pl.BlockSpec(subln.shape, lambda b, h, i: (0, 0))],
        out_specs=pl.BlockSpec((1, blk, blkw), lambda b, h, i: (b, i, h)),
        out_shape=jax.ShapeDtypeStruct((n, t, DIFF_W), BF16),
        compiler_params=_cparams("arbitrary", "arbitrary", "arbitrary"),
        name="diff_prompt_attention",
    )(dq, dk, dv, k_tab, q_tab, diff_lambda, subln)


PAGES_PER_CHUNK = 16
N_SLOTS = 2


def _sample_attn_kernel(lam_init, layer, n_chunks, pt_ref,
                        qrow_ref, dqrow_ref, nckv_ref, nkpe_ref, ndk_ref, ndv_ref,
                        lat_hbm, kpe_hbm, dk_hbm, dv_hbm,
                        wkb_ref, wv_ref, kg_ref, lp_ref, sub_ref,
                        om_ref, od_ref,
                        lat_buf, kpe_buf, dk_buf, dv_buf, sems):
    b = pl.program_id(0)
    nb = pl.num_programs(0)
    ppc = PAGES_PER_CHUNK
    n_new = nckv_ref.shape[1]
    rows = MLA_HEADS * n_new
    hrows = PAGE_SIZE * DIFF_HEADS

    def copies(seq, chunk, slot):
        out = []
        for p in range(ppc):
            page = pt_ref[seq, chunk * ppc + p]
            out.append(pltpu.make_async_copy(lat_hbm.at[layer, page], lat_buf.at[slot, p], sems.at[0, slot]))
            out.append(pltpu.make_async_copy(kpe_hbm.at[layer, page], kpe_buf.at[slot, p], sems.at[1, slot]))
            out.append(pltpu.make_async_copy(dk_hbm.at[layer, page], dk_buf.at[slot, pl.ds(p * hrows, hrows)],
                                             sems.at[2, slot]))
            out.append(pltpu.make_async_copy(dv_hbm.at[layer, page], dv_buf.at[slot, pl.ds(p * hrows, hrows)],
                                             sems.at[3, slot]))
        return out

    @pl.when(b == 0)
    def _():
        for c in copies(0, 0, 0):
            c.start()

    qrow = qrow_ref[0]
    qt = qrow * (kg_ref[...] * (QK_DIM ** -0.5))
    lane = _lane_iota(qt.shape)
    qn = jnp.where(lane < QK_NOPE, qt, 0.0)
    qn2 = qn + pltpu.roll(qn, QK_NOPE, 1)
    qbd = jnp.concatenate([qn2] * (MLA_HEADS * QK_NOPE // LANES), axis=1)
    col = _lane_iota(qbd.shape)
    row = lax.broadcasted_iota(jnp.int32, qbd.shape, 0)
    blockdiag = col // QK_NOPE == row // n_new
    qbd = jnp.where(blockdiag, qbd, 0.0)
    qlat = _dot_nt(qbd, wkb_ref[...]).astype(BF16)
    qrope = pltpu.roll(qt, LANES - QK_NOPE, 1)[:, :QK_ROPE].astype(BF16)
    ind = jnp.where(blockdiag, 1.0, 0.0).astype(BF16)

    dqrow = dqrow_ref[0] * (DIFF_HD ** -0.5)
    dq2 = jnp.concatenate([dqrow, dqrow], axis=1)
    dqbd = jnp.concatenate([dq2] * (DIFF_W // LANES), axis=1)
    dqbd = jnp.where(blockdiag, dqbd, 0.0).astype(BF16)
    hd_col = lax.broadcasted_iota(jnp.int32, (rows, 1), 0) // (2 * n_new) + 1
    slope_col = jnp.exp2(-8.0 * hd_col.astype(F32) / DIFF_HEADS)

    def softmax_step(s, m, l):
        m_new = jnp.maximum(m, jnp.max(s, axis=-1, keepdims=True))
        alpha = jnp.exp(m - m_new)
        p = jnp.exp(s - m_new)
        return p, alpha, m_new, alpha * l + jnp.sum(p, axis=-1, keepdims=True)

    def chunk_update(state, ckv, kpe_t, dk, dv, kpos, mask):
        (m_a, l_a, acc_a), (m_d, l_d, acc_d) = state
        ckv_b = ckv.astype(BF16)
        kn = jnp.dot(ckv_b, wkb_ref[...], preferred_element_type=F32)
        ss = _dot_nt(ind, kn * kn) + jnp.sum(kpe_t * kpe_t, axis=0, keepdims=True)
        r = lax.rsqrt(ss * (1.0 / QK_DIM) + EPS)
        s = (_dot_nt(qlat, ckv_b) + _dot(qrope, kpe_t)) * r
        if mask is not None:
            s = jnp.where(mask, s, -jnp.inf)
        p, alpha, m_a, l_a = softmax_step(s, m_a, l_a)
        acc_a = acc_a * alpha + _dot(p, ckv_b)

        sd = _dot_nt(dqbd, dk) + slope_col * kpos
        if mask is not None:
            sd = jnp.where(mask, sd, -jnp.inf)
        pd, alpha_d, m_d, l_d = softmax_step(sd, m_d, l_d)
        acc_d = acc_d * alpha_d + _dot(pd, dv)
        return (m_a, l_a, acc_a), (m_d, l_d, acc_d)

    def init(width):
        return (jnp.full((rows, 1), -jnp.inf, F32), jnp.zeros((rows, 1), F32), jnp.zeros((rows, width), F32))

    def heads_to_lanes(buf, slot, n):
        return jnp.concatenate([buf[slot, pl.ds(h, n, stride=DIFF_HEADS), :] for h in range(DIFF_HEADS)], axis=1)

    state = (init(KV_LORA), init(DIFF_W))
    kc = ppc * PAGE_SIZE
    kpos_c = lax.broadcasted_iota(jnp.int32, (1, kc), 1).astype(F32)
    for c in range(n_chunks):
        slot = c % N_SLOTS
        nxt = (c + 1) % N_SLOTS
        if c + 1 < n_chunks:
            for cp in copies(b, c + 1, nxt):
                cp.start()
        else:
            @pl.when(b + 1 < nb)
            def _():
                for cp in copies(b + 1, 0, nxt):
                    cp.start()
        for cp in copies(b, c, slot):
            cp.wait()
        kpe_t = jnp.concatenate([kpe_buf[slot, p] for p in range(ppc)], axis=1)
        state = chunk_update(
            state,
            lat_buf[slot].reshape(kc, KV_LORA), kpe_t,
            heads_to_lanes(dk_buf, slot, kc), heads_to_lanes(dv_buf, slot, kc),
            kpos_c + float(c * kc), None)

    past = n_chunks * kc
    key_t = lax.broadcasted_iota(jnp.int32, (rows, n_new), 1)
    qry_t = lax.broadcasted_iota(jnp.int32, (rows, n_new), 0) % n_new
    kpos_n = lax.broadcasted_iota(jnp.int32, (1, n_new), 1).astype(F32) + float(past)
    state = chunk_update(state, nckv_ref[0], nkpe_ref[0], ndk_ref[0], ndv_ref[0], kpos_n, key_t <= qry_t)

    (_, l_a, acc_a), (_, l_d, acc_d) = state
    ov = _dot(acc_a / l_a, wv_ref[...])
    om = jnp.concatenate([ov[h * n_new:(h + 1) * n_new, h * V_HEAD:(h + 1) * V_HEAD]
                          for h in range(MLA_HEADS)], axis=1)
    om_ref[0] = om.astype(om_ref.dtype)

    lam = _diff_lambda(lp_ref, lam_init)
    on = acc_d / l_d
    ods = []
    for h in range(DIFF_HEADS):
        r0 = (2 * h) * n_new
        o0 = on[r0:r0 + n_new, h * LANES:(h + 1) * LANES]
        o1 = on[r0 + n_new:r0 + 2 * n_new, h * LANES:(h + 1) * LANES]
        o = o0 - lam * o1
        ods.append(_rms(o, 2 * DIFF_HD) * sub_ref[...] * (1.0 - lam_init))
    od_ref[0] = jnp.concatenate(ods, axis=1).astype(od_ref.dtype)


def _sample_attention(layer, page_table, qrow, dqrow, nckv, nkpe, ndk, ndv, lat, kpe, dk, dv,
                      wkb, wv, k_gain, diff_lambda, subln, lam_init):
    nseq, n_pages = page_table.shape
    n_new = nckv.shape[1]
    n_chunks = n_pages // PAGES_PER_CHUNK
    ppc = PAGES_PER_CHUNK
    seq = lambda a: pl.BlockSpec((1,) + a.shape[1:], lambda b, pt: (b,) + (0,) * (a.ndim - 1))
    full = lambda a: pl.BlockSpec(a.shape, lambda b, pt: (0,) * a.ndim)
    hbm = pl.BlockSpec(memory_space=pl.ANY)
    grid_spec = pltpu.PrefetchScalarGridSpec(
        num_scalar_prefetch=1,
        grid=(nseq,),
        in_specs=[seq(qrow), seq(dqrow), seq(nckv), seq(nkpe), seq(ndk), seq(ndv), hbm, hbm, hbm, hbm,
                  full(wkb), full(wv), full(k_gain), full(diff_lambda), full(subln)],
        out_specs=[pl.BlockSpec((1, n_new, MLA_OUT), lambda b, pt: (b, 0, 0)),
                   pl.BlockSpec((1, n_new, DIFF_W), lambda b, pt: (b, 0, 0))],
        scratch_shapes=[pltpu.VMEM((N_SLOTS, ppc, PAGE_SIZE, KV_LORA), F32),
                        pltpu.VMEM((N_SLOTS, ppc, QK_ROPE, PAGE_SIZE), F32),
                        pltpu.VMEM((N_SLOTS, ppc * PAGE_SIZE * DIFF_HEADS, 2 * DIFF_HD), F32),
                        pltpu.VMEM((N_SLOTS, ppc * PAGE_SIZE * DIFF_HEADS, 2 * DIFF_HD), F32),
                        pltpu.SemaphoreType.DMA((4, N_SLOTS))],
    )
    return pl.pallas_call(
        functools.partial(_sample_attn_kernel, lam_init, layer, n_chunks),
        grid_spec=grid_spec,
        out_shape=[jax.ShapeDtypeStruct((nseq, n_new, MLA_OUT), BF16),
                   jax.ShapeDtypeStruct((nseq, n_new, DIFF_W), BF16)],
        compiler_params=_cparams("arbitrary"),
        name="sample_attention",
    )(page_table, qrow, dqrow, nckv, nkpe, ndk, ndv, lat, kpe, dk, dv, wkb, wv, k_gain, diff_lambda, subln)


def _merge_kernel(x_ref, om_ref, od_ref, wo_ref, gate1_ref, g2_ref, sh2_ref, sc2_ref, rw_ref, rb_ref,
                  x1_ref, h2_ref, gates_ref):
    nb, s, d = x_ref.shape
    om = om_ref[...].reshape(nb * s, MLA_OUT)
    od = od_ref[...].reshape(nb * s, DIFF_W)
    y = _dot(om, wo_ref[:MLA_OUT, :]) + _dot(od, wo_ref[MLA_OUT:, :])
    _residual_and_route(x_ref[...], y, gate1_ref[...], g2_ref[...], sh2_ref[...], sc2_ref[...],
                        rw_ref[...], rb_ref[...], x1_ref, h2_ref, gates_ref)


def _route_outputs(nb_tot, s_tot, d, tok, tm, gate_index):
    return ([jax.ShapeDtypeStruct((nb_tot, s_tot, d), F32),
             jax.ShapeDtypeStruct((nb_tot, s_tot, d), BF16),
             jax.ShapeDtypeStruct((N_EXPERTS, nb_tot * s_tot), F32)],
            [tok(d), tok(d), pl.BlockSpec((N_EXPERTS, tm), gate_index)])


def _attn_merge(x3, om, od, w_o, gate1, g2, sh2, sc2, rw, rb, nb, s):
    nb_tot, s_tot, d = x3.shape
    tok = lambda w: pl.BlockSpec((nb, s, w), lambda i, j: (i, j, 0))
    mod = pl.BlockSpec((nb, 1, d), lambda i, j: (i, 0, 0))
    full = lambda a: pl.BlockSpec(a.shape, lambda i, j: (0,) * a.ndim)
    nj = s_tot // s
    out_shape, out_specs = _route_outputs(nb_tot, s_tot, d, tok, nb * s, lambda i, j: (0, i * nj + j))
    return pl.pallas_call(
        _merge_kernel,
        grid=(nb_tot // nb, nj),
        in_specs=[tok(d), tok(MLA_OUT), tok(DIFF_W), full(w_o), mod, full(g2), mod, mod, full(rw), full(rb)],
        out_specs=out_specs,
        out_shape=out_shape,
        compiler_params=_cparams("arbitrary", "arbitrary"),
        name="attn_merge",
    )(x3, om, od, w_o, gate1, g2, sh2, sc2, rw, rb)


CONV_TAIL = 32
SUBLANES = 8
CONV_ROWS = 32


def _conv_post(y, cn_ref, pw2_ref, pw2b_ref):
    y = _rms(y, y.shape[-1]) * cn_ref[...]
    y = y * jax.nn.sigmoid(y)
    return _dot(y, pw2_ref[...]) + pw2b_ref[...]


def _glu(h, pw1_ref, pw1b_ref):
    z = _dot(h, pw1_ref[...]) + pw1b_ref[...]
    c = z.shape[-1] // 2
    return z[:, :c] * jax.nn.sigmoid(z[:, c:])


def _conv_prompt_kernel(x_ref, sh_ref, sc_ref, g1_ref, pw1_ref, pw1b_ref, dw_ref, dwb_ref, cn_ref,
                        pw2_ref, pw2b_ref, gate1_ref, g2_ref, sh2_ref, sc2_ref, rw_ref, rb_ref,
                        x1_ref, h2_ref, gates_ref, st_ref, ubuf, shifted, ybuf):
    j = pl.program_id(1)
    _, s, d = x_ref.shape
    off = CONV_TAIL - (CONV_W - 1)
    span = CONV_TAIL + s - SUBLANES

    @pl.when(j == 0)
    def _():
        ubuf[0:CONV_TAIL, :] = jnp.zeros((CONV_TAIL, d), F32)

    @pl.when(j > 0)
    def _():
        ubuf[0:CONV_TAIL, :] = ubuf[s:s + CONV_TAIL, :]

    h = _modulate(x_ref[...], g1_ref[...], sh_ref[...], sc_ref[...])
    ubuf[CONV_TAIL:CONV_TAIL + s, :] = _glu(h, pw1_ref, pw1b_ref)
    st_ref[0, 0] = ubuf[s:s + CONV_TAIL, :]
    for b in range(SUBLANES):
        n_b = CONV_TAIL + s if b == 0 else span
        shifted[b, 0:n_b, :] = ubuf[b:b + n_b, :]

    def rows_body(c, carry):
        r0 = pl.multiple_of(c * CONV_ROWS, CONV_ROWS)
        acc = jnp.zeros((CONV_ROWS, d), F32) + dwb_ref[...]
        for w in range(CONV_W):
            a, b = divmod(off + w, SUBLANES)
            acc = acc + shifted[b, pl.ds(r0 + a * SUBLANES, CONV_ROWS), :] * dw_ref[w:w + 1, :]
        ybuf[pl.ds(r0, CONV_ROWS), :] = acc
        return carry

    lax.fori_loop(0, s // CONV_ROWS, rows_body, 0)
    y = _conv_post(ybuf[...], cn_ref, pw2_ref, pw2b_ref)
    _residual_and_route(x_ref[...], y, gate1_ref[...], g2_ref[...], sh2_ref[...], sc2_ref[...],
                        rw_ref[...], rb_ref[...], x1_ref, h2_ref, gates_ref)


def _conv_prompt(x3, shift, scale, g1, cw, gate1, g2, sh2, sc2, rw, rb, s):
    n, t, d = x3.shape
    nt = t // s
    tok = lambda w: pl.BlockSpec((1, s, w), lambda i, j: (i, j, 0))
    mod = pl.BlockSpec((1, 1, d), lambda i, j: (i, 0, 0))
    full = lambda a: pl.BlockSpec(a.shape, lambda i, j: (0,) * a.ndim)
    out_shape, out_specs = _route_outputs(n, t, d, tok, s, lambda i, j: (0, i * nt + j))
    out_shape.append(jax.ShapeDtypeStruct((n, nt, CONV_TAIL, d), F32))
    out_specs.append(pl.BlockSpec((1, 1, CONV_TAIL, d), lambda i, j: (i, j, 0, 0)))
    cargs = (cw["pw1"], cw["pw1_b"], cw["dw"], cw["dw_b"], cw["norm"], cw["pw2"], cw["pw2_b"])
    x1, h2, gates, tails = pl.pallas_call(
        _conv_prompt_kernel,
        grid=(n, nt),
        in_specs=[tok(d), mod, mod, full(g1)] + [full(a) for a in cargs]
                 + [mod, full(g2), mod, mod, full(rw), full(rb)],
        out_specs=out_specs,
        out_shape=out_shape,
        scratch_shapes=[pltpu.VMEM((CONV_TAIL + s, d), F32),
                        pltpu.VMEM((SUBLANES, CONV_TAIL + s, d), F32),
                        pltpu.VMEM((s, d), F32)],
        compiler_params=_cparams("arbitrary", "arbitrary"),
        name="conv_prompt",
    )(x3, shift, scale, g1, *cargs, gate1, g2, sh2, sc2, rw, rb)
    return x1, h2, gates, tails[:, nt - 1, CONV_TAIL - (CONV_W - 1):, :]


def _conv_sample_kernel(x_ref, st_ref, sh_ref, sc_ref, g1_ref, pw1_ref, pw1b_ref, dw_ref, dwb_ref, cn_ref,
                        pw2_ref, pw2b_ref, gate1_ref, g2_ref, sh2_ref, sc2_ref, rw_ref, rb_ref,
                        x1_ref, h2_ref, gates_ref, nst_ref, full_buf):
    s, nb, d = x_ref.shape
    hist = CONV_W - 1
    x = x_ref[...]
    y = _rms(x, d) * g1_ref[...][None]
    h = (y * (1.0 + sc_ref[...]) + sh_ref[...]).reshape(s * nb, d)
    full_buf[0:hist] = st_ref[...]
    full_buf[hist:hist + s] = _glu(h, pw1_ref, pw1b_ref).reshape(s, nb, d)
    acc = jnp.zeros((s, nb, d), F32) + dwb_ref[...][None]
    for w in range(CONV_W):
        acc = acc + full_buf[w:w + s] * dw_ref[w:w + 1, :][None]
    nst_ref[...] = full_buf[s:s + hist]
    yc = _conv_post(acc.reshape(s * nb, d), cn_ref, pw2_ref, pw2b_ref).reshape(s, nb, d)
    x1 = x + gate1_ref[...] * yc
    x1_ref[...] = x1
    y2 = _rms(x1, d) * g2_ref[...][None]
    h2 = (y2 * (1.0 + sc2_ref[...]) + sh2_ref[...]).reshape(s * nb, d)
    h2_ref[...] = h2.astype(BF16).reshape(s, nb, d)
    gates_ref[...] = _router_gates_t(h2, rw_ref[...], rb_ref[...])


def _conv_sample(x_t, state_t, shift, scale, g1, cw, gate1, g2, sh2, sc2, rw, rb, nb):
    s, nseq, d = x_t.shape
    hist = CONV_W - 1
    tok = lambda w: pl.BlockSpec((s, nb, w), lambda i: (0, i, 0))
    mod = pl.BlockSpec((1, nb, d), lambda i: (0, i, 0))
    st = pl.BlockSpec((hist, nb, d), lambda i: (0, i, 0))
    full = lambda a: pl.BlockSpec(a.shape, lambda i: (0,) * a.ndim)
    cargs = (cw["pw1"], cw["pw1_b"], cw["dw"], cw["dw_b"], cw["norm"], cw["pw2"], cw["pw2_b"])
    return pl.pallas_call(
        _conv_sample_kernel,
        grid=(nseq // nb,),
        in_specs=[tok(d), st, mod, mod, full(g1)] + [full(a) for a in cargs]
                 + [mod, full(g2), mod, mod, full(rw), full(rb)],
        out_specs=[tok(d), tok(d), pl.BlockSpec((N_EXPERTS, s * nb), lambda i: (0, i)), st],
        out_shape=[jax.ShapeDtypeStruct((s, nseq, d), F32),
                   jax.ShapeDtypeStruct((s, nseq, d), BF16),
                   jax.ShapeDtypeStruct((N_EXPERTS, s * nseq), F32),
                   jax.ShapeDtypeStruct((hist, nseq, d), F32)],
        scratch_shapes=[pltpu.VMEM((hist + s, nb, d), F32)],
        compiler_params=_cparams("arbitrary"),
        name="conv_sample",
    )(x_t, state_t, shift, scale, g1, *cargs, gate1, g2, sh2, sc2, rw, rb)


def _moe_kernel(h_ref, gates_ref, x1_ref, gate2_ref, wg_ref, wu_ref, wd_ref, o_ref, acc_ref):
    e = pl.program_id(2)
    ne = pl.num_programs(2)
    nb, s, d = h_ref.shape
    tm = nb * s

    @pl.when(e == 0)
    def _():
        acc_ref[...] = jnp.zeros_like(acc_ref)

    h = h_ref[...].reshape(tm, d)
    a = _dot(h, wg_ref[0])
    u = (a * jax.nn.sigmoid(a)) * _dot(h, wu_ref[0])
    gates = gates_ref[...].reshape(tm, N_EXPERTS)
    ge = jnp.sum(jnp.where(_lane_iota(gates.shape) == e, gates, 0.0), axis=-1, keepdims=True)
    acc_ref[...] += ge * _dot(u, wd_ref[0])

    @pl.when(e == ne - 1)
    def _():
        o_ref[...] = x1_ref[...] + gate2_ref[...] * acc_ref[...].reshape(nb, s, d)


def _moe(h2, gates, x1, gate2, wg, wu, wd, nb, s):
    nb_tot, s_tot, d = h2.shape
    ne, _, de = wg.shape
    tok = lambda w: pl.BlockSpec((nb, s, w), lambda i, j, e: (i, j, 0))
    mod = pl.BlockSpec((nb, 1, d), lambda i, j, e: (i, 0, 0))
    return pl.pallas_call(
        _moe_kernel,
        grid=(nb_tot // nb, s_tot // s, ne),
        in_specs=[tok(d), tok(N_EXPERTS), tok(d), mod,
                  pl.BlockSpec((1, d, de), lambda i, j, e: (e, 0, 0)),
                  pl.BlockSpec((1, d, de), lambda i, j, e: (e, 0, 0)),
                  pl.BlockSpec((1, de, d), lambda i, j, e: (e, 0, 0))],
        out_specs=tok(d),
        out_shape=jax.ShapeDtypeStruct((nb_tot, s_tot, d), F32),
        scratch_shapes=[pltpu.VMEM((nb * s, d), F32)],
        compiler_params=_cparams("arbitrary", "arbitrary", "arbitrary"),
        name="moe",
    )(h2, gates, x1, gate2, wg, wu, wd)


MOE_TILE = 512
META_ROWS = 8


def _route_meta_kernel(g_ref, eid_ref, meta_ref, cnt_ref, run_ref):
    i = pl.program_id(0)
    tm = g_ref.shape[1]

    @pl.when(i == 0)
    def _():
        run_ref[...] = jnp.zeros_like(run_ref)

    hot = g_ref[...] > 0.0
    hotf = jnp.where(hot, 1.0, 0.0)
    before = lax.broadcasted_iota(jnp.int32, (tm, tm), 0) < lax.broadcasted_iota(jnp.int32, (tm, tm), 1)
    rank = _dot(hotf, jnp.where(before, 1.0, 0.0)) + run_ref[...]
    eid = eid_ref[...]
    e0 = jnp.min(jnp.where(hot, eid, float(N_EXPERTS - 1)), axis=0, keepdims=True)
    e1 = jnp.max(jnp.where(hot, eid, 0.0), axis=0, keepdims=True)
    p0 = jnp.sum(jnp.where(jnp.logical_and(hot, eid == e0), rank, 0.0), axis=0, keepdims=True)
    p1 = jnp.sum(jnp.where(jnp.logical_and(hot, eid == e1), rank, 0.0), axis=0, keepdims=True)
    rec = jnp.concatenate([e0, e1, p0, p1, jnp.zeros((META_ROWS - 4, tm), F32)], axis=0)
    meta_ref[...] = rec.astype(jnp.int32)
    run_ref[...] += jnp.sum(hotf, axis=1, keepdims=True)
    cnt_ref[...] = run_ref[...]


def _route_meta(gates_t, eid, tm):
    e, m = gates_t.shape
    return pl.pallas_call(
        _route_meta_kernel,
        grid=(m // tm,),
        in_specs=[pl.BlockSpec((e, tm), lambda i: (0, i)), pl.BlockSpec((e, 1), lambda i: (0, 0))],
        out_specs=[pl.BlockSpec((META_ROWS, tm), lambda i: (0, i)), pl.BlockSpec((e, 1), lambda i: (0, 0))],
        out_shape=[jax.ShapeDtypeStruct((META_ROWS, m), jnp.int32), jax.ShapeDtypeStruct((e, 1), F32)],
        scratch_shapes=[pltpu.VMEM((e, 1), F32)],
        compiler_params=_cparams("arbitrary"),
        name="route_meta",
    )(gates_t, eid)


def _row_copies(src_ref, dst_ref, rows_ref, sem, gather):
    n = rows_ref.shape[2]

    def body(i, carry):
        for k in range(SUBLANES):
            hbm_row = pl.ds(rows_ref[0, 0, i * SUBLANES + k], 1)
            if gather:
                pltpu.make_async_copy(src_ref.at[hbm_row, :], dst_ref.at[i, pl.ds(k, 1), :], sem).start()
            else:
                pltpu.make_async_copy(src_ref.at[i, pl.ds(k, 1), :], dst_ref.at[hbm_row, :], sem).start()
        return carry

    lax.fori_loop(0, n // SUBLANES, body, 0)


def _wait_rows(hbm_ref, vmem_ref, sem, gather):
    def body(i, carry):
        if gather:
            pltpu.make_async_copy(hbm_ref.at[pl.ds(0, SUBLANES), :], vmem_ref.at[i], sem).wait()
        else:
            pltpu.make_async_copy(vmem_ref.at[i], hbm_ref.at[pl.ds(0, SUBLANES), :], sem).wait()
        return carry

    lax.fori_loop(0, vmem_ref.shape[0], body, 0)


def _dispatch_kernel(d0_ref, d1_ref, h_ref, zero_ref, xs_ref, buf, sems):
    del zero_ref
    buf[...] = h_ref[0].astype(F32).reshape(buf.shape)
    _row_copies(buf, xs_ref, d0_ref, sems.at[0], gather=False)
    _row_copies(buf, xs_ref, d1_ref, sems.at[1], gather=False)
    _wait_rows(xs_ref, buf, sems.at[0], gather=False)
    _wait_rows(xs_ref, buf, sems.at[1], gather=False)


def _dispatch(h2, d0, d1, rows_pad, tm):
    n, t, d = h2.shape
    nt = t // tm
    smem = pl.BlockSpec((1, 1, tm), lambda i, j: (i * nt + j, 0, 0), memory_space=pltpu.SMEM)
    hbm = pl.BlockSpec(memory_space=pl.ANY)
    return pl.pallas_call(
        _dispatch_kernel,
        grid=(n, nt),
        in_specs=[smem, smem, pl.BlockSpec((1, tm, d), lambda i, j: (i, j, 0)), hbm],
        out_specs=hbm,
        out_shape=jax.ShapeDtypeStruct((rows_pad, d), F32),
        scratch_shapes=[pltpu.VMEM((tm // SUBLANES, SUBLANES, d), F32), pltpu.SemaphoreType.DMA((2,))],
        input_output_aliases={3: 0},
        compiler_params=_cparams("arbitrary", "arbitrary"),
        name="moe_dispatch",
    )(d0, d1, h2, jnp.zeros((rows_pad, d), F32))


def _expert_tile_kernel(te_ref, tv_ref, x_ref, wg_ref, wu_ref, wd_ref, y_ref):
    del te_ref
    t = pl.program_id(0)

    @pl.when(tv_ref[t] != 0)
    def _():
        x = x_ref[...]
        a = _dot(x, wg_ref[0])
        u = (a * jax.nn.sigmoid(a)) * _dot(x, wu_ref[0])
        y_ref[...] = _dot(u, wd_ref[0])

    @pl.when(tv_ref[t] == 0)
    def _():
        y_ref[...] = jnp.zeros_like(y_ref)


def _expert_tiles(xs, tile_expert, tile_valid, wg, wu, wd):
    rows, d = xs.shape
    _, _, de = wg.shape
    grid_spec = pltpu.PrefetchScalarGridSpec(
        num_scalar_prefetch=2,
        grid=(rows // MOE_TILE,),
        in_specs=[pl.BlockSpec((MOE_TILE, d), lambda t, te, tv: (t, 0)),
                  pl.BlockSpec((1, d, de), lambda t, te, tv: (te[t], 0, 0)),
                  pl.BlockSpec((1, d, de), lambda t, te, tv: (te[t], 0, 0)),
                  pl.BlockSpec((1, de, d), lambda t, te, tv: (te[t], 0, 0))],
        out_specs=pl.BlockSpec((MOE_TILE, d), lambda t, te, tv: (t, 0)),
    )
    return pl.pallas_call(
        _expert_tile_kernel,
        grid_spec=grid_spec,
        out_shape=jax.ShapeDtypeStruct((rows, d), F32),
        compiler_params=_cparams("arbitrary"),
        name="moe_expert_tiles",
    )(tile_expert, tile_valid, xs, wg, wu, wd)


def _combine_kernel(d0_ref, d1_ref, y_ref, x1_ref, gate2_ref, gates_ref, o_ref, b0, b1, sems):
    _row_copies(y_ref, b0, d0_ref, sems.at[0], gather=True)
    _row_copies(y_ref, b1, d1_ref, sems.at[1], gather=True)
    g = gates_ref[0]
    lane = _lane_iota(g.shape).astype(F32)
    first = jnp.min(jnp.where(g > 0.0, lane, float(N_EXPERTS)), axis=-1, keepdims=True)
    w0 = jnp.sum(jnp.where(lane == first, g, 0.0), axis=-1, keepdims=True)
    w1 = jnp.sum(g, axis=-1, keepdims=True) - w0
    _wait_rows(y_ref, b0, sems.at[0], gather=True)
    _wait_rows(y_ref, b1, sems.at[1], gather=True)
    tm, d = x1_ref.shape[1:]
    o_ref[0] = x1_ref[0] + gate2_ref[0] * (w0 * b0[...].reshape(tm, d) + w1 * b1[...].reshape(tm, d))


def _combine(y, d0, d1, x1, gate2, gates, tm):
    n, t, d = x1.shape
    nt = t // tm
    smem = pl.BlockSpec((1, 1, tm), lambda i, j: (i * nt + j, 0, 0), memory_space=pltpu.SMEM)
    tok = lambda w: pl.BlockSpec((1, tm, w), lambda i, j: (i, j, 0))
    return pl.pallas_call(
        _combine_kernel,
        grid=(n, nt),
        in_specs=[smem, smem, pl.BlockSpec(memory_space=pl.ANY), tok(d),
                  pl.BlockSpec((1, 1, d), lambda i, j: (i, 0, 0)), tok(N_EXPERTS)],
        out_specs=tok(d),
        out_shape=jax.ShapeDtypeStruct((n, t, d), F32),
        scratch_shapes=[pltpu.VMEM((tm // SUBLANES, SUBLANES, d), F32),
                        pltpu.VMEM((tm // SUBLANES, SUBLANES, d), F32), pltpu.SemaphoreType.DMA((2,))],
        compiler_params=_cparams("arbitrary", "arbitrary"),
        name="moe_combine",
    )(d0, d1, y, x1, gate2, gates)


def _moe_sorted(h2, gates_t, x1, gate2, wg, wu, wd, tm):
    n, t, d = h2.shape
    m = n * t
    rows = jnp.arange(N_EXPERTS)
    eid = (rows % N_GROUPS) * E_PER_GROUP + rows // N_GROUPS
    meta, cnt_rows = _route_meta(gates_t, eid.astype(F32)[:, None], tm)
    row_of_expert = (rows % E_PER_GROUP) * N_GROUPS + rows // E_PER_GROUP
    cnt = cnt_rows[:, 0].astype(jnp.int32)[row_of_expert]
    tiles = (cnt + MOE_TILE - 1) // MOE_TILE
    tile_end = jnp.cumsum(tiles)
    start_row = (tile_end - tiles) * MOE_TILE
    n_tiles = (2 * m) // MOE_TILE + N_EXPERTS
    tile_id = jnp.arange(n_tiles)
    tile_valid = (tile_id < tile_end[-1]).astype(jnp.int32)
    tile_expert = jnp.minimum(jnp.sum(tile_id[:, None] >= tile_end[None, :], axis=1), N_EXPERTS - 1).astype(jnp.int32)
    d0 = (start_row[meta[0]] + meta[2]).reshape(m // tm, 1, tm)
    d1 = (start_row[meta[1]] + meta[3]).reshape(m // tm, 1, tm)
    xs = _dispatch(h2, d0, d1, n_tiles * MOE_TILE, tm)
    y = _expert_tiles(xs, tile_expert, tile_valid, wg, wu, wd)
    return _combine(y, d0, d1, x1, gate2, _gates_dense(gates_t).reshape(n, t, N_EXPERTS), tm)


def _tied(g):
    return jnp.concatenate([g, g[QK_NOPE:], jnp.zeros((HEAD_PAD - QK_DIM,), F32)])[None]


def _rope_tables(pos):
    half = QK_ROPE // 2
    inv = ROPE_BASE ** (-jnp.arange(0, QK_ROPE, 2, dtype=F32) / QK_ROPE)
    ang = pos.astype(F32)[:, None] * inv[None, :]
    cos, sin = jnp.cos(ang), jnp.sin(ang)
    t = pos.shape[0]
    z = lambda n: jnp.zeros((t, n), F32)
    one = jnp.ones((t, QK_NOPE), F32)
    q_cos = jnp.concatenate([one, cos, cos, z(HEAD_PAD - QK_DIM)], axis=1)
    q_sa = jnp.concatenate([z(QK_NOPE), -sin, z(half), z(HEAD_PAD - QK_DIM)], axis=1)
    q_sb = jnp.concatenate([z(QK_NOPE), z(half), sin, z(HEAD_PAD - QK_DIM)], axis=1)
    k_cos = jnp.concatenate([cos, cos, z(LANES - QK_ROPE)], axis=1)
    k_sa = jnp.concatenate([-sin, z(LANES - half)], axis=1)
    k_sb = jnp.concatenate([z(half), sin, z(LANES - QK_ROPE)], axis=1)
    return jnp.stack([q_cos, q_sa, q_sb, k_cos, k_sa, k_sb])


def _attn_weights(w_in, q_a_norm, w_q_b, kv_a_norm, w_kv_b, mla_q_gain, mla_k_gain, diff_q_gain, diff_k_gain):
    d = w_in.shape[0]
    c1 = Q_LORA + KV_LORA
    w_in_p = jnp.concatenate([w_in[:, :c1], w_in[:, c1:c1 + QK_ROPE], jnp.zeros((d, LANES - QK_ROPE), F32),
                              w_in[:, c1 + QK_ROPE:]], axis=1).astype(BF16)
    wqb = jnp.pad(w_q_b.reshape(Q_LORA, MLA_HEADS, QK_DIM), ((0, 0), (0, 0), (0, HEAD_PAD - QK_DIM)))
    wkv = w_kv_b.reshape(KV_LORA, MLA_HEADS, QK_NOPE + V_HEAD)
    wkn = jnp.pad(wkv[:, :, :QK_NOPE], ((0, 0), (0, 0), (0, HEAD_PAD - QK_NOPE)))
    wv = wkv[:, :, QK_NOPE:].reshape(KV_LORA, MLA_OUT)
    return {
        "w_in": w_in_p,
        "q_a_norm": q_a_norm[None],
        "w_q_b": wqb.reshape(Q_LORA, MLA_HEADS * HEAD_PAD).astype(BF16),
        "kv_a_norm": kv_a_norm[None],
        "w_kv": jnp.concatenate([wkn.reshape(KV_LORA, MLA_HEADS * HEAD_PAD), wv], axis=1).astype(BF16),
        "w_kb": wkv[:, :, :QK_NOPE].reshape(KV_LORA, MLA_HEADS * QK_NOPE).astype(BF16),
        "w_v": wv.astype(BF16),
        "q_gain": _tied(mla_q_gain),
        "k_gain": _tied(mla_k_gain),
        "dq_gain": jnp.concatenate([diff_q_gain, diff_q_gain])[None],
        "dk_gain": jnp.concatenate([diff_k_gain, diff_k_gain])[None],
    }


PROMPT_TILE = 512
SAMPLE_SEQS = 64


def kernel(x_prompt, x_sample, cache_mla_latent, cache_mla_kpe, cache_diff_k, cache_diff_v, state_conv, page_table, c_prompt, c_sample, norm1, norm2, ada_w, ada_b, w_in, q_a_norm, w_q_b, kv_a_norm, w_kv_b, mla_q_gain, mla_k_gain, diff_q_gain, diff_k_gain, diff_lambda, diff_subln, w_o, conv_pw1, conv_pw1_b, conv_dw, conv_dw_b, conv_norm, conv_pw2, conv_pw2_b, router_w, router_bias, moe_w_gate, moe_w_up, moe_w_down):
    n, t, d = x_prompt.shape
    nseq, n_new, _ = x_sample.shape
    n_pages = page_table.shape[1]
    depth = norm1.shape[0]
    sp = min(PROMPT_TILE, t)
    sb = min(SAMPLE_SEQS, nseq)

    rows = -(-(n + nseq) // 8) * 8
    c_all = jnp.concatenate([c_prompt, c_sample, jnp.zeros((rows - n - nseq, d), F32)], axis=0)
    ada = _adaln(c_all, ada_w, ada_b).reshape(depth, rows, 6, d)

    router_w, rb = _router_layout(router_w, router_bias)
    xp, xs = x_prompt, x_sample
    outs = {k: [] for k in ("lat_p", "lat_s", "kpe_p", "kpe_s", "dk_p", "dk_s", "dv_p", "dv_s", "cv_p", "cv_s")}
    for l in range(depth):
        mp = [ada[l, :n, i][:, None, :] for i in range(6)]
        ms = [ada[l, n:n + nseq, i][:, None, :] for i in range(6)]
        g1, g2 = norm1[l][None], norm2[l][None]
        wg, wu, wd = moe_w_gate[l], moe_w_up[l], moe_w_down[l]
        if l % 2 == 0:
            a = l // 2
            lam_init = 0.8 - 0.6 * math.exp(-0.3 * l)
            wts = _attn_weights(w_in[a], q_a_norm[a], w_q_b[a], kv_a_norm[a], w_kv_b[a], mla_q_gain[a],
                                mla_k_gain[a], diff_q_gain[a], diff_k_gain[a])
            subln = diff_subln[a][None]
            w_o_b = w_o[a].astype(BF16)
            tab_p = _rope_tables(jnp.arange(t))
            q, ckv, kpe, dq, dk, dv, k, v, dkb, dvb = _attn_project(True, xp, mp[0], mp[1], g1, wts, tab_p, 1, sp)
            om = _mla_prompt_attention(q, k, v)
            od = _diff_prompt_attention(dq, dkb, dvb, diff_lambda[a], subln, lam_init)
            xp1, hp2, gp = _attn_merge(xp, om, od, w_o_b, mp[2], g2, mp[3], mp[4], router_w, rb, 1, sp)
            outs["lat_p"].append(ckv); outs["kpe_p"].append(kpe)
            outs["dk_p"].append(dk.reshape(n, t, DIFF_HEADS, 2 * DIFF_HD))
            outs["dv_p"].append(dv.reshape(n, t, DIFF_HEADS, 2 * DIFF_HD))
            tab_s = _rope_tables(n_pages * PAGE_SIZE + jnp.arange(n_new))
            q, ckv, kpe, dq, dk, dv = _attn_project(False, xs, ms[0], ms[1], g1, wts, tab_s, sb, n_new)
            qrow = q.reshape(nseq, n_new, MLA_HEADS, HEAD_PAD).transpose(0, 2, 1, 3).reshape(
                nseq, MLA_HEADS * n_new, HEAD_PAD)
            dqrow = dq.reshape(nseq, n_new, 2 * DIFF_HEADS, DIFF_HD).transpose(0, 2, 1, 3).reshape(
                nseq, 2 * DIFF_HEADS * n_new, DIFF_HD)
            hp_shape = cache_diff_k.shape[:2] + (PAGE_SIZE * DIFF_HEADS, 2 * DIFF_HD)
            om, od = _sample_attention(
                a, page_table, qrow, dqrow, ckv, kpe.transpose(0, 2, 1), dk, dv,
                cache_mla_latent, jnp.swapaxes(cache_mla_kpe, 2, 3),
                cache_diff_k.reshape(hp_shape), cache_diff_v.reshape(hp_shape),
                wts["w_kb"], wts["w_v"], wts["k_gain"], diff_lambda[a], subln, lam_init)
            xs1, hs2, gs = _attn_merge(xs, om, od, w_o_b, ms[2], g2, ms[3], ms[4], router_w, rb, sb, n_new)
            outs["lat_s"].append(ckv); outs["kpe_s"].append(kpe)
            outs["dk_s"].append(dk.reshape(nseq, n_new, DIFF_HEADS, 2 * DIFF_HD))
            outs["dv_s"].append(dv.reshape(nseq, n_new, DIFF_HEADS, 2 * DIFF_HD))
        else:
            c = l // 2
            cw = {"pw1": conv_pw1[c].astype(BF16), "pw1_b": conv_pw1_b[c][None], "dw": conv_dw[c],
                  "dw_b": conv_dw_b[c][None], "norm": conv_norm[c][None], "pw2": conv_pw2[c].astype(BF16),
                  "pw2_b": conv_pw2_b[c][None]}
            xp1, hp2, gp, st_p = _conv_prompt(xp, mp[0], mp[1], g1, cw, mp[2], g2, mp[3], mp[4], router_w, rb,
                                              min(256, t))
            outs["cv_p"].append(st_p)
            tm_ = lambda m: m.transpose(1, 0, 2)
            xs1_t, hs2_t, gs_t, st_s = _conv_sample(
                xs.transpose(1, 0, 2), state_conv[c].transpose(1, 0, 2), tm_(ms[0]), tm_(ms[1]), g1, cw,
                tm_(ms[2]), g2, tm_(ms[3]), tm_(ms[4]), router_w, rb, min(32, nseq))
            xs1, hs2 = xs1_t.transpose(1, 0, 2), hs2_t.transpose(1, 0, 2)
            nbc = min(32, nseq)
            gs = gs_t.reshape(N_EXPERTS, nseq // nbc, n_new, nbc).transpose(0, 1, 3, 2).reshape(N_EXPERTS, -1)
            outs["cv_s"].append(st_s.transpose(1, 0, 2))
        xp = _moe_sorted(hp2, gp, xp1, mp[5], wg, wu, wd, min(MOE_TILE, t))
        xs = _moe(hs2, _gates_dense(gs).reshape(nseq, n_new, N_EXPERTS), xs1, ms[5], wg, wu, wd, nseq, n_new)
    st = lambda k: jnp.stack(outs[k])
    return (xp, xs, st("lat_p"), st("lat_s"), st("kpe_p"), st("kpe_s"), st("dk_p"), st("dk_s"),
            st("dv_p"), st("dv_s"), st("cv_p"), st("cv_s"))
```
